```python
import jax, jax.numpy as jnp
from jax import lax
import numpy as np

D_MODEL = 4096
BATCH = 2
SEQ = 4096
DEPTH = 2

HGRN_HEADS = 16
HGRN_HEAD_DIM = 128
HGRN_WIDTH = HGRN_HEADS * HGRN_HEAD_DIM
CHUNK = 64
POOL_WINDOWS = (2, 4, 8, 16)
POOL_GROUPS = len(POOL_WINDOWS)
POOL_GROUP_DIM = 512
POOL_WIDTH = POOL_GROUPS * POOL_GROUP_DIM
IN_COLS = 4 * HGRN_WIDTH + POOL_WIDTH + 2 * D_MODEL
IN_SPLITS = (HGRN_WIDTH, 2 * HGRN_WIDTH, 3 * HGRN_WIDTH, 4 * HGRN_WIDTH,
             4 * HGRN_WIDTH + POOL_WIDTH, 4 * HGRN_WIDTH + POOL_WIDTH + D_MODEL)
N_GROUPS = 4
EXPERTS_PER_GROUP = 8
EXPERT_TOP_K = 2
D_EXPERT = 768
RMS_EPS = 1e-6
MIN_FORGET = 1e-30

kernel_name = "hgrn2_pool_gated_hier_moe_trunk"


def rmsnorm(x, g):
    xf = x.astype(jnp.float32)
    y = xf * lax.rsqrt(jnp.mean(xf * xf, axis=-1, keepdims=True) + RMS_EPS)
    return (y * g.astype(jnp.float32)).astype(x.dtype)


def _to_chunks(t):
    b, s, h, d = t.shape
    return t.reshape(b, s // CHUNK, CHUNK, h, d).transpose(1, 0, 3, 2, 4)


def hgrn2_chunkwise(q, k, v, log_f):
    B, S, H, DK = q.shape
    DV = v.shape[-1]
    causal = jnp.tril(jnp.ones((CHUNK, CHUNK), dtype=bool))[None, None, :, :, None]

    def step(state, inp):
        q_, k_, v_, lf = inp
        b = jnp.cumsum(lf, axis=2)
        diff = b[:, :, :, None, :] - b[:, :, None, :, :]
        decay = jnp.where(causal, jnp.exp(jnp.minimum(diff, 0.0)), 0.0)
        scores = jnp.einsum('bhtd,bhsd,bhtsd->bhts', q_, k_, decay)
        o = jnp.einsum('bhts,bhsv->bhtv', scores, v_) + \
            jnp.einsum('bhtd,bhdv->bhtv', q_ * jnp.exp(b), state)
        b_last = b[:, :, -1:, :]
        state = jnp.exp(b_last[:, :, 0, :])[..., None] * state + \
            jnp.einsum('bhsd,bhsv->bhdv', k_ * jnp.exp(b_last - b), v_)
        return state, o

    state0 = jnp.zeros((B, H, DK, DV), jnp.float32)
    _, o = lax.scan(step, state0, (_to_chunks(q), _to_chunks(k), _to_chunks(v), _to_chunks(log_f)))
    return o.transpose(1, 0, 3, 2, 4).reshape(B, S, H, DV)


def token_mixer(h, w_in, lb, hgrn_norm_g, w_pool, pool_scale, w_branch_a, w_branch_b, w_out):
    f32 = jnp.float32
    B, S, _ = h.shape
    proj = h @ w_in
    q, f_pre, i_in, og, u, ga, gb = jnp.split(proj, IN_SPLITS, axis=-1)

    z = f_pre.astype(f32)
    lbf = lb.astype(f32)
    f = lbf + (1.0 - lbf) * jax.nn.sigmoid(z)
    log_f = jnp.log(jnp.maximum(f, MIN_FORGET))
    k = (1.0 - lbf) * jax.nn.sigmoid(-z)
    hs = (B, S, HGRN_HEADS, HGRN_HEAD_DIM)
    o = hgrn2_chunkwise(q.astype(f32).reshape(hs), k.reshape(hs),
                        i_in.astype(f32).reshape(hs), log_f.reshape(hs))
    o = o * lax.rsqrt(jnp.mean(o * o, axis=-1, keepdims=True) + RMS_EPS) * hgrn_norm_g.astype(f32)
    o = o * jax.nn.silu(og.astype(f32).reshape(hs))
    o_a = o.reshape(B, S, HGRN_WIDTH).astype(h.dtype)

    uf = u.astype(f32).reshape(B, S, POOL_GROUPS, POOL_GROUP_DIM)
    cs = jnp.cumsum(uf, axis=1)
    pos = jnp.arange(1, S + 1, dtype=f32)
    means = []
    for g, w in enumerate(POOL_WINDOWS):
        c = cs[:, :, g]
        lag = jnp.pad(c[:, :S - w], ((0, 0), (w, 0), (0, 0)))
        means.append((c - lag) / jnp.minimum(pos, float(w))[None, :, None])
    pooled = jnp.stack(means, axis=2) - uf
    mixed = jnp.einsum('bsgc,gcd->bsgd', pooled, w_pool.astype(f32))
    o_b = (mixed.reshape(B, S, POOL_WIDTH) * pool_scale.astype(f32)).astype(h.dtype)

    merged = jax.nn.sigmoid(ga) * (o_a @ w_branch_a) + jax.nn.sigmoid(gb) * (o_b @ w_branch_b)
    return merged @ w_out


def hier_moe(h, w_router_group, w_router_expert, w_gate, w_up, w_down):
    f32 = jnp.float32
    B, S, D = h.shape
    t = h.reshape(B * S, D)
    g_logits = (t @ w_router_group).astype(f32)
    g_probs = jax.nn.softmax(g_logits, axis=-1)
    g_idx = jnp.argmax(g_logits, axis=-1)
    g_w = jnp.take_along_axis(g_probs, g_idx[:, None], axis=-1)
    e_logits = (t @ w_router_expert).astype(f32).reshape(-1, N_GROUPS, EXPERTS_PER_GROUP)
    e_logits = jnp.take_along_axis(e_logits, g_idx[:, None, None], axis=1)[:, 0]
    top_v, top_i = lax.top_k(e_logits, EXPERT_TOP_K)
    e_w = jax.nn.softmax(top_v, axis=-1) * g_w
    e_combine = jnp.sum(jax.nn.one_hot(top_i, EXPERTS_PER_GROUP, dtype=f32) * e_w[..., None], axis=1)
    combine = jax.nn.one_hot(g_idx, N_GROUPS, dtype=f32)[:, :, None] * e_combine[:, None, :]
    out = jnp.zeros((B * S, D), f32)
    for g in range(N_GROUPS):
        a = jnp.einsum('nd,edf->nef', t, w_gate[g])
        b = jnp.einsum('nd,edf->nef', t, w_up[g])
        hid = jax.nn.silu(a) * b * combine[:, g, :, None].astype(t.dtype)
        out = out + jnp.einsum('nef,efd->nd', hid, w_down[g]).astype(f32)
    return out.astype(h.dtype).reshape(B, S, D)


def setup_inputs(seed: int = 0) -> dict:
    key = jax.random.key(seed)
    ks = jax.random.split(key, 20)
    f32 = jnp.float32
    nrm = lambda k, shape, scale: jax.random.normal(k, shape, f32) * scale
    L = DEPTH
    return {
        "x": nrm(ks[0], (BATCH, SEQ, D_MODEL), 1.0),
        "norm1_g": 1.0 + nrm(ks[1], (L, D_MODEL), 0.1),
        "w_in": nrm(ks[2], (L, D_MODEL, IN_COLS), D_MODEL ** -0.5),
        "lb_logits": nrm(ks[3], (L, HGRN_WIDTH), 0.5),
        "hgrn_norm_g": 1.0 + nrm(ks[4], (L, HGRN_HEAD_DIM), 0.1),
        "w_pool": nrm(ks[5], (L, POOL_GROUPS, POOL_GROUP_DIM, POOL_GROUP_DIM), POOL_GROUP_DIM ** -0.5),
        "pool_scale": 1.0 + nrm(ks[6], (L, POOL_WIDTH), 0.1),
        "w_branch_a": nrm(ks[7], (L, HGRN_WIDTH, D_MODEL), HGRN_WIDTH ** -0.5),
        "w_branch_b": nrm(ks[8], (L, POOL_WIDTH, D_MODEL), POOL_WIDTH ** -0.5),
        "w_out": nrm(ks[9], (L, D_MODEL, D_MODEL), D_MODEL ** -0.5),
        "norm2_g": 1.0 + nrm(ks[10], (L, D_MODEL), 0.1),
        "w_router_group": nrm(ks[11], (L, D_MODEL, N_GROUPS), D_MODEL ** -0.5),
        "w_router_expert": nrm(ks[12], (L, D_MODEL, N_GROUPS * EXPERTS_PER_GROUP), D_MODEL ** -0.5),
        "w_gate": nrm(ks[13], (L, N_GROUPS, EXPERTS_PER_GROUP, D_MODEL, D_EXPERT), D_MODEL ** -0.5),
        "w_up": nrm(ks[14], (L, N_GROUPS, EXPERTS_PER_GROUP, D_MODEL, D_EXPERT), D_MODEL ** -0.5),
        "w_down": nrm(ks[15], (L, N_GROUPS, EXPERTS_PER_GROUP, D_EXPERT, D_MODEL), D_EXPERT ** -0.5),
        "final_norm_g": 1.0 + nrm(ks[16], (D_MODEL,), 0.1),
    }


def reference(x, norm1_g, w_in, lb_logits, hgrn_norm_g, w_pool, pool_scale, w_branch_a,
              w_branch_b, w_out, norm2_g, w_router_group, w_router_expert, w_gate, w_up,
              w_down, final_norm_g):
    p = jax.nn.softmax(lb_logits.astype(jnp.float32), axis=0)
    lower_bounds = jnp.cumsum(p, axis=0) - p[0:1]
    for l in range(DEPTH):
        x = x + token_mixer(rmsnorm(x, norm1_g[l]), w_in[l], lower_bounds[l], hgrn_norm_g[l],
                            w_pool[l], pool_scale[l], w_branch_a[l], w_branch_b[l], w_out[l])
        x = x + hier_moe(rmsnorm(x, norm2_g[l]), w_router_group[l], w_router_expert[l],
                         w_gate[l], w_up[l], w_down[l])
    return rmsnorm(x, final_norm_g)
```

```python
import functools

import jax
import jax.numpy as jnp
from jax import lax
from jax.experimental import pallas as pl
from jax.experimental.pallas import tpu as pltpu

RMS_EPS = 1e-6
MIN_FORGET = 1e-30
POOL_WINDOWS = (2, 4, 8, 16)
POOL_HALO = 16
HGRN_CHUNK = 64
HGRN_SUB = 16
LANES = 128
VMEM_LIMIT_BYTES = 56 * 1024 * 1024

F32 = jnp.float32
BF16 = jnp.bfloat16


def _params(*sem):
    return pltpu.CompilerParams(dimension_semantics=sem, vmem_limit_bytes=VMEM_LIMIT_BYTES)


def _tile(n, pref):
    t = min(n, pref)
    assert n % t == 0, (n, pref)
    return t


def _sigmoid(z):
    return 1.0 / (1.0 + jnp.exp(-z))


def _dot(a, b):
    return jnp.dot(a, b, preferred_element_type=F32)


def _dot_nt(a, b):
    return lax.dot_general(a, b, (((1,), (1,)), ((), ())), preferred_element_type=F32)


def _rms(x, g):
    return x * lax.rsqrt(jnp.mean(x * x, axis=-1, keepdims=True) + RMS_EPS) * g


def _rmsnorm_kernel(x_ref, g_ref, o_ref):
    o_ref[...] = _rms(x_ref[...], g_ref[...]).astype(o_ref.dtype)


def _rmsnorm(x, g, out_dtype):
    n, d = x.shape
    tr = _tile(n, 256)
    return pl.pallas_call(
        _rmsnorm_kernel,
        out_shape=jax.ShapeDtypeStruct((n, d), out_dtype),
        grid=(n // tr,),
        in_specs=[pl.BlockSpec((tr, d), lambda i: (i, 0)),
                  pl.BlockSpec((1, d), lambda i: (0, 0))],
        out_specs=pl.BlockSpec((tr, d), lambda i: (i, 0)),
        compiler_params=_params("parallel"),
        name="rmsnorm",
    )(x, g.reshape(1, d))


def _inproj_kernel(h_ref, w_ref, o_ref):
    o_ref[...] = _dot(h_ref[...], w_ref[...].astype(BF16))


def _inproj(h, w, layer):
    n, d = h.shape
    cols = w.shape[2]
    bm, bn = _tile(n, 1024), _tile(cols, 512)
    return pl.pallas_call(
        _inproj_kernel,
        out_shape=jax.ShapeDtypeStruct((n, cols), F32),
        grid=(n // bm, cols // bn),
        in_specs=[pl.BlockSpec((bm, d), lambda i, j: (i, 0)),
                  pl.BlockSpec((None, d, bn), lambda i, j: (layer, 0, j))],
        out_specs=pl.BlockSpec((bm, bn), lambda i, j: (i, j)),
        compiler_params=_params("parallel", "arbitrary"),
        name="inproj",
    )(h, w)


def _split3(x):
    hi = x.astype(BF16)
    r1 = x - hi.astype(F32)
    mid = r1.astype(BF16)
    lo = (r1 - mid.astype(F32)).astype(BF16)
    return hi, mid, lo


def _hgrn_kernel(lbl_ref, g_ref, q_ref, f_ref, i_ref, og_ref, o_ref,
                 st_ref, kb_ref, bb_ref, vb_ref, *, layer, n_chunks):
    C, SB = HGRN_CHUNK, HGRN_SUB
    dk = q_ref.shape[1]

    @pl.when(pl.program_id(2) == 0)
    def _():
        st_ref[...] = jnp.zeros_like(st_ref)

    zpad = jnp.zeros((SB, dk), F32)
    kb_ref[0:SB, :] = zpad
    bb_ref[0:SB, :] = zpad
    vb_ref[0:SB, :] = zpad

    lbl = lbl_ref[...]
    e = jnp.exp(lbl - jnp.max(lbl, axis=0, keepdims=True))
    p = e / jnp.sum(e, axis=0, keepdims=True)
    lb = jnp.sum(p[0:layer + 1], axis=0, keepdims=True) - p[0:1]
    gain = g_ref[...]

    r_i = lax.broadcasted_iota(jnp.int32, (C, C), 0)
    c_i = lax.broadcasted_iota(jnp.int32, (C, C), 1)
    tri = jnp.where(r_i >= c_i, 1.0, 0.0).astype(BF16)
    tmod = lax.broadcasted_iota(jnp.int32, (C, dk), 0) % SB

    def chunk(c, carry):
        r0 = pl.multiple_of(c * C, C)
        rows = pl.ds(r0, C)
        z = f_ref[rows, :]
        q = q_ref[rows, :]
        v = i_ref[rows, :]
        f = lb + (1.0 - lb) * _sigmoid(z)
        lf = jnp.log(jnp.maximum(f, MIN_FORGET))
        k = (1.0 - lb) * _sigmoid(-z)
        hi, mid, lo = _split3(lf)
        b = _dot(tri, hi) + _dot(tri, mid) + _dot(tri, lo)

        st = st_ref[...]
        o_inter = _dot_nt((q * jnp.exp(b)).astype(BF16), st.astype(BF16))

        pieces = [jnp.zeros((SB, dk), F32)]
        for i in range(1, C // SB):
            lo_r, hi_r = i * SB, (i + 1) * SB
            ref = b[lo_r - 1:lo_r, :]
            qt = q[lo_r:hi_r] * jnp.exp(jnp.minimum(b[lo_r:hi_r] - ref, 0.0))
            kt = k[:lo_r] * jnp.exp(jnp.minimum(ref - b[:lo_r], 0.0))
            a = _dot_nt(qt.astype(BF16), kt.astype(BF16))
            pieces.append(_dot(a.astype(BF16), v[:lo_r].astype(BF16)))
        o = o_inter + jnp.concatenate(pieces, axis=0)

        kb_ref[SB:SB + C, :] = k
        bb_ref[SB:SB + C, :] = b
        vb_ref[SB:SB + C, :] = v
        for j in range(SB):
            kj = kb_ref[SB - j:SB - j + C, :]
            bj = bb_ref[SB - j:SB - j + C, :]
            vj = vb_ref[SB - j:SB - j + C, :]
            pj = q * kj * jnp.exp(jnp.minimum(b - bj, 0.0))
            aj = jnp.sum(pj, axis=-1, keepdims=True)
            o = o + jnp.where(tmod >= j, aj, 0.0) * vj

        b_last = b[C - 1:C, :]
        khat = k * jnp.exp(b_last - b)
        st_ref[...] = st * jnp.exp(b_last) + _dot(v.T.astype(BF16), khat.astype(BF16))

        og = og_ref[rows, :]
        y = _rms(o, gain) * (og * _sigmoid(og))
        o_ref[rows, :] = y.astype(o_ref.dtype)
        return carry

    lax.fori_loop(0, n_chunks, chunk, 0)


def _hgrn(proj, lb_logits, gain, *, layer, batch, seq, width):
    n = proj.shape[0]
    dk = gain.shape[-1]
    heads = width // dk
    t = _tile(seq, 512)
    ns = seq // t
    n_layers = lb_logits.shape[0]

    def col(seg):
        return pl.BlockSpec((t, dk), lambda b, h, s, seg=seg: (b * ns + s, seg * heads + h))

    kern = functools.partial(_hgrn_kernel, layer=layer, n_chunks=t // HGRN_CHUNK)
    return pl.pallas_call(
        kern,
        out_shape=jax.ShapeDtypeStruct((n, width), BF16),
        grid=(batch, heads, ns),
        in_specs=[pl.BlockSpec((n_layers, dk), lambda b, h, s: (0, h)),
                  pl.BlockSpec((1, dk), lambda b, h, s: (0, 0)),
                  col(0), col(1), col(2), col(3)],
        out_specs=pl.BlockSpec((t, dk), lambda b, h, s: (b * ns + s, h)),
        scratch_shapes=[pltpu.VMEM((dk, dk), F32),
                        pltpu.VMEM((HGRN_SUB + HGRN_CHUNK, dk), F32),
                        pltpu.VMEM((HGRN_SUB + HGRN_CHUNK, dk), F32),
                        pltpu.VMEM((HGRN_SUB + HGRN_CHUNK, dk), F32)],
        compiler_params=_params("parallel", "parallel", "arbitrary"),
        name="hgrn2",
    )(lb_logits, gain.reshape(1, dk), proj, proj, proj, proj)


def _pool_kernel(u_ref, halo_ref, w_ref, sc_ref, o_ref, *, windows):
    g = pl.program_id(2)
    s = pl.program_id(1)
    t = u_ref.shape[0]
    u = u_ref[...]
    win = jnp.int32(windows[0])
    for gi in range(1, len(windows)):
        win = jnp.where(g == gi, jnp.int32(windows[gi]), win)
    r_i = lax.broadcasted_iota(jnp.int32, (t, POOL_HALO + t), 0)
    c_i = lax.broadcasted_iota(jnp.int32, (t, POOL_HALO + t), 1) - POOL_HALO
    lag = r_i - c_i
    inside = (lag >= 0) & (lag < win) & (c_i + s * t >= 0)
    band = jnp.where(inside, 1.0, 0.0).astype(BF16)
    ext = jnp.concatenate([halo_ref[...].astype(BF16), u.astype(BF16)], axis=0)
    wsum = _dot(band, ext)
    pos = (lax.broadcasted_iota(jnp.int32, (t, 1), 0) + s * t + 1).astype(F32)
    pooled = wsum / jnp.minimum(pos, win.astype(F32)) - u
    mixed = _dot(pooled.astype(BF16), w_ref[...].astype(BF16))
    o_ref[...] = (mixed * sc_ref[...]).astype(o_ref.dtype)


def _pool(proj, w_pool, pool_scale, *, layer, batch, seq, col0):
    n = proj.shape[0]
    _, groups, cg, _ = w_pool.shape
    t = _tile(seq, 512)
    ns = seq // t
    hb = t // POOL_HALO
    assert col0 % cg == 0 and groups == len(POOL_WINDOWS)
    cb = col0 // cg
    kern = functools.partial(_pool_kernel, windows=POOL_WINDOWS)
    return pl.pallas_call(
        kern,
        out_shape=jax.ShapeDtypeStruct((n, groups * cg), BF16),
        grid=(batch, ns, groups),
        in_specs=[pl.BlockSpec((t, cg), lambda b, s, g: (b * ns + s, cb + g)),
                  pl.BlockSpec((POOL_HALO, cg),
                               lambda b, s, g: (jnp.maximum((b * ns + s) * hb - 1, 0), cb + g)),
                  pl.BlockSpec((None, None, cg, cg), lambda b, s, g: (layer, g, 0, 0)),
                  pl.BlockSpec((1, cg), lambda b, s, g: (0, g))],
        out_specs=pl.BlockSpec((t, cg), lambda b, s, g: (b * ns + s, g)),
        compiler_params=_params("parallel", "parallel", "arbitrary"),
        name="pool_mixer",
    )(proj, proj, w_pool, pool_scale[layer].reshape(1, groups * cg))


def _merge_kernel(oa_ref, ob_ref, wa_ref, wb_ref, ga_ref, gb_ref, o_ref):
    a = _dot(oa_ref[...], wa_ref[...].astype(BF16))
    b = _dot(ob_ref[...], wb_ref[...].astype(BF16))
    o_ref[...] = (_sigmoid(ga_ref[...]) * a + _sigmoid(gb_ref[...]) * b).astype(o_ref.dtype)


def _merge(o_a, o_b, w_a, w_b, proj, *, layer, col_ga, col_gb):
    n, wa = o_a.shape
    wb = o_b.shape[1]
    d = w_a.shape[2]
    bm, bn = _tile(n, 1024), _tile(d, 512)
    assert col_ga % bn == 0 and col_gb % bn == 0
    ca, cb = col_ga // bn, col_gb // bn
    return pl.pallas_call(
        _merge_kernel,
        out_shape=jax.ShapeDtypeStruct((n, d), BF16),
        grid=(n // bm, d // bn),
        in_specs=[pl.BlockSpec((bm, wa), lambda i, j: (i, 0)),
                  pl.BlockSpec((bm, wb), lambda i, j: (i, 0)),
                  pl.BlockSpec((None, wa, bn), lambda i, j: (layer, 0, j)),
                  pl.BlockSpec((None, wb, bn), lambda i, j: (layer, 0, j)),
                  pl.BlockSpec((bm, bn), lambda i, j: (i, ca + j)),
                  pl.BlockSpec((bm, bn), lambda i, j: (i, cb + j))],
        out_specs=pl.BlockSpec((bm, bn), lambda i, j: (i, j)),
        compiler_params=_params("parallel", "arbitrary"),
        name="gated_merge",
    )(o_a, o_b, w_a, w_b, proj, proj)


def _outproj_kernel(m_ref, w_ref, x_ref, o_ref):
    o_ref[...] = x_ref[...] + _dot(m_ref[...], w_ref[...].astype(BF16))


def _outproj(merged, w, x, layer):
    n, d = merged.shape
    dn = w.shape[2]
    bm, bn = _tile(n, 1024), _tile(dn, 512)
    return pl.pallas_call(
        _outproj_kernel,
        out_shape=jax.ShapeDtypeStruct((n, dn), F32),
        grid=(n // bm, dn // bn),
        in_specs=[pl.BlockSpec((bm, d), lambda i, j: (i, 0)),
                  pl.BlockSpec((None, d, bn), lambda i, j: (layer, 0, j)),
                  pl.BlockSpec((bm, bn), lambda i, j: (i, j))],
        out_specs=pl.BlockSpec((bm, bn), lambda i, j: (i, j)),
        compiler_params=_params("parallel", "arbitrary"),
        name="outproj_residual",
    )(merged, w, x)


def _router_kernel(x_ref, g_ref, wr_ref, hp_ref, route_ref, cnt_ref, run_ref, *, n_groups, n_experts):
    tr, d = x_ref.shape
    half = d // 2

    @pl.when(pl.program_id(0) == 0)
    def _():
        run_ref[...] = jnp.zeros_like(run_ref)

    h = _rms(x_ref[...], g_ref[...])
    hb = h.astype(BF16)
    lo = pltpu.bitcast(hb[:, :half].astype(F32), jnp.uint32) >> 16
    hi = pltpu.bitcast(hb[:, half:].astype(F32), jnp.uint32) & jnp.uint32(0xFFFF0000)
    hp_ref[...] = lo | hi

    h_lo = (h - hb.astype(F32)).astype(BF16)
    wr = wr_ref[...]
    w_hi = wr.astype(BF16)
    w_lo = (wr - w_hi.astype(F32)).astype(BF16)
    logits = _dot(hb, w_hi) + _dot(hb, w_lo) + _dot(h_lo, w_hi)

    lane = lax.broadcasted_iota(jnp.int32, (tr, LANES), 1).astype(F32)
    ninf = jnp.float32(-jnp.inf)
    big = jnp.float32(1e9)
    is_g = lane < n_groups
    gl = jnp.where(is_g, logits, ninf)
    gmax = jnp.max(gl, axis=-1, keepdims=True)
    gidx = jnp.min(jnp.where(gl == gmax, lane, big), axis=-1, keepdims=True)
    g_w = 1.0 / jnp.sum(jnp.where(is_g, jnp.exp(gl - gmax), 0.0), axis=-1, keepdims=True)

    e_lo = n_groups + gidx * n_experts
    el = jnp.where(lane >= e_lo, jnp.where(lane < e_lo + n_experts, logits, ninf), ninf)
    v1 = jnp.max(el, axis=-1, keepdims=True)
    i1 = jnp.min(jnp.where(el == v1, lane, big), axis=-1, keepdims=True)
    el2 = jnp.where(lane == i1, ninf, el)
    v2 = jnp.max(el2, axis=-1, keepdims=True)
    i2 = jnp.min(jnp.where(el2 == v2, lane, big), axis=-1, keepdims=True)
    tt = jnp.exp(v2 - v1)
    w1 = g_w / (1.0 + tt)
    w2 = g_w * tt / (1.0 + tt)
    e1 = i1 - n_groups
    e2 = i2 - n_groups

    oh1 = lane == e1
    oh2 = lane == e2
    cnt = jnp.where(oh1, 1.0, 0.0) + jnp.where(oh2, 1.0, 0.0)
    r_i = lax.broadcasted_iota(jnp.int32, (tr, tr), 0)
    c_i = lax.broadcasted_iota(jnp.int32, (tr, tr), 1)
    strict = jnp.where(r_i > c_i, 1.0, 0.0).astype(BF16)
    before = _dot(strict, cnt.astype(BF16)) + run_ref[0:1, :]
    rank1 = jnp.sum(jnp.where(oh1, before, 0.0), axis=-1, keepdims=True)
    rank2 = jnp.sum(jnp.where(oh2, before, 0.0), axis=-1, keepdims=True)
    run = run_ref[0:1, :] + jnp.sum(cnt, axis=0, keepdims=True)
    run_ref[...] = jnp.broadcast_to(run, run_ref.shape)
    cnt_ref[...] = jnp.broadcast_to(run, cnt_ref.shape)

    out = jnp.zeros((tr, LANES), F32)
    for idx, val in enumerate((e1, e2, w1, w2, rank1, rank2)):
        out = jnp.where(lane == idx, val, out)
    route_ref[...] = out


def _router(x, g, w_router, *, n_groups, n_experts):
    n, d = x.shape
    tr = _tile(n, 256)
    kern = functools.partial(_router_kernel, n_groups=n_groups, n_experts=n_experts)
    return pl.pallas_call(
        kern,
        out_shape=(jax.ShapeDtypeStruct((n, d // 2), jnp.uint32),
                   jax.ShapeDtypeStruct((n, LANES), F32),
                   jax.ShapeDtypeStruct((8, LANES), F32)),
        grid=(n // tr,),
        in_specs=[pl.BlockSpec((tr, d), lambda i: (i, 0)),
                  pl.BlockSpec((1, d), lambda i: (0, 0)),
                  pl.BlockSpec((d, LANES), lambda i: (0, 0))],
        out_specs=(pl.BlockSpec((tr, d // 2), lambda i: (i, 0)),
                   pl.BlockSpec((tr, LANES), lambda i: (i, 0)),
                   pl.BlockSpec((8, LANES), lambda i: (0, 0))),
        scratch_shapes=[pltpu.VMEM((8, LANES), F32)],
        compiler_params=_params("arbitrary"),
        name="norm_router",
    )(x, g.reshape(1, d), w_router)


def _row_copy(src_ref, src_row, dst_ref, dst_row, sem):
    return pltpu.make_async_copy(src_ref.at[pl.ds(src_row, 1)], dst_ref.at[pl.ds(dst_row, 1)], sem)


def _dispatch_kernel(d1_ref, d2_ref, h_ref, xs_in_ref, xs_ref, sem):
    del xs_in_ref
    tr = h_ref.shape[0]
    base = pl.program_id(0) * tr

    def issue(r, carry):
        _row_copy(h_ref, r, xs_ref, d1_ref[base + r], sem.at[0]).start()
        _row_copy(h_ref, r, xs_ref, d2_ref[base + r], sem.at[1]).start()
        return carry

    lax.fori_loop(0, tr, issue, 0)

    def drain(r, carry):
        _row_copy(h_ref, r, xs_ref, d1_ref[base + r], sem.at[0]).wait()
        _row_copy(h_ref, r, xs_ref, d2_ref[base + r], sem.at[1]).wait()
        return carry

    lax.fori_loop(0, tr, drain, 0)


def _dispatch(hp, dest1, dest2, rows):
    n, dw = hp.shape
    tr = _tile(n, 256)
    grid_spec = pltpu.PrefetchScalarGridSpec(
        num_scalar_prefetch=2,
        grid=(n // tr,),
        in_specs=[pl.BlockSpec((tr, dw), lambda i, d1, d2: (i, 0)),
                  pl.BlockSpec(memory_space=pl.ANY)],
        out_specs=pl.BlockSpec(memory_space=pl.ANY),
        scratch_shapes=[pltpu.SemaphoreType.DMA((2,))],
    )
    return pl.pallas_call(
        _dispatch_kernel,
        out_shape=jax.ShapeDtypeStruct((rows, dw), hp.dtype),
        grid_spec=grid_spec,
        input_output_aliases={3: 0},
        compiler_params=_params("arbitrary"),
        name="dispatch",
    )(dest1, dest2, hp, jnp.zeros((rows, dw), hp.dtype))


def _expert_kernel(te_ref, tb_ref, nv_ref, xs_ref, wg_ref, wu_ref, wd_ref, o_ref):
    del te_ref, tb_ref
    i, j = pl.program_id(0), pl.program_id(1)
    half = xs_ref.shape[1]

    @pl.when(jnp.logical_and(i >= nv_ref[0], j == 0))
    def _():
        o_ref[...] = jnp.zeros_like(o_ref)

    @pl.when(i < nv_ref[0])
    def _():
        xw = xs_ref[...]
        x_lo = pltpu.bitcast(xw << 16, F32).astype(BF16)
        x_hi = pltpu.bitcast(xw & jnp.uint32(0xFFFF0000), F32).astype(BF16)

        def up(w_ref):
            return (_dot(x_lo, w_ref[0:half, :].astype(BF16)) +
                    _dot(x_hi, w_ref[half:, :].astype(BF16)))

        a = up(wg_ref)
        b = up(wu_ref)
        hid = (a * _sigmoid(a) * b).astype(BF16)
        y = _dot(hid, wd_ref[...].astype(BF16))

        @pl.when(j == 0)
        def _():
            o_ref[...] = y

        @pl.when(j > 0)
        def _():
            o_ref[...] += y


def _experts(xs, w_gate, w_up, w_down, tile_expert, tile_block, n_valid, *, tm):
    rows, dw = xs.shape
    _, d, f = w_gate.shape
    fc = _tile(f, 256)
    nfc = f // fc

    def jj(i, j, nv):
        return jnp.where(i < nv[0], j, nfc - 1)

    grid_spec = pltpu.PrefetchScalarGridSpec(
        num_scalar_prefetch=3,
        grid=(rows // tm, nfc),
        in_specs=[pl.BlockSpec((tm, dw), lambda i, j, te, tb, nv: (tb[i], 0)),
                  pl.BlockSpec((None, d, fc), lambda i, j, te, tb, nv: (te[i], 0, jj(i, j, nv))),
                  pl.BlockSpec((None, d, fc), lambda i, j, te, tb, nv: (te[i], 0, jj(i, j, nv))),
                  pl.BlockSpec((None, fc, d), lambda i, j, te, tb, nv: (te[i], jj(i, j, nv), 0))],
        out_specs=pl.BlockSpec((tm, d), lambda i, j, te, tb, nv: (i, 0)),
    )
    return pl.pallas_call(
        _expert_kernel,
        out_shape=jax.ShapeDtypeStruct((rows, d), F32),
        grid_spec=grid_spec,
        compiler_params=_params("arbitrary", "arbitrary"),
        name="expert_mlp",
    )(tile_expert, tile_block, n_valid, xs, w_gate, w_up, w_down)


def _combine_kernel(d1_ref, d2_ref, x_ref, route_ref, g_ref, ys_ref, *rest, final):
    if final:
        y_ref, buf1, buf2, sem = rest
    else:
        x2_ref, hn_ref, buf1, buf2, sem = rest
    tr = x_ref.shape[0]
    base = pl.program_id(0) * tr

    def issue(r, carry):
        _row_copy(ys_ref, d1_ref[base + r], buf1, r, sem.at[0]).start()
        _row_copy(ys_ref, d2_ref[base + r], buf2, r, sem.at[1]).start()
        return carry

    lax.fori_loop(0, tr, issue, 0)

    def drain(r, carry):
        _row_copy(ys_ref, d1_ref[base + r], buf1, r, sem.at[0]).wait()
        _row_copy(ys_ref, d2_ref[base + r], buf2, r, sem.at[1]).wait()
        return carry

    lax.fori_loop(0, tr, drain, 0)

    route = route_ref[...]
    x2 = x_ref[...] + route[:, 2:3] * buf1[...] + route[:, 3:4] * buf2[...]
    if final:
        y_ref[...] = _rms(x2, g_ref[...])
    else:
        x2_ref[...] = x2
        hn_ref[...] = _rms(x2, g_ref[...]).astype(hn_ref.dtype)


def _combine(x, route, g, ys, dest1, dest2, *, final):
    n, d = x.shape
    tr = _tile(n, 256)
    row = lambda i, d1, d2: (i, 0)
    if final:
        out_shape = jax.ShapeDtypeStruct((n, d), F32)
        out_specs = pl.BlockSpec((tr, d), row)
    else:
        out_shape = (jax.ShapeDtypeStruct((n, d), F32), jax.ShapeDtypeStruct((n, d), BF16))
        out_specs = (pl.BlockSpec((tr, d), row), pl.BlockSpec((tr, d), row))
    grid_spec = pltpu.PrefetchScalarGridSpec(
        num_scalar_prefetch=2,
        grid=(n // tr,),
        in_specs=[pl.BlockSpec((tr, d), row),
                  pl.BlockSpec((tr, LANES), row),
                  pl.BlockSpec((1, d), lambda i, d1, d2: (0, 0)),
                  pl.BlockSpec(memory_space=pl.ANY)],
        out_specs=out_specs,
        scratch_shapes=[pltpu.VMEM((tr, d), F32), pltpu.VMEM((tr, d), F32),
                        pltpu.SemaphoreType.DMA((2,))],
    )
    return pl.pallas_call(
        functools.partial(_combine_kernel, final=final),
        out_shape=out_shape,
        grid_spec=grid_spec,
        compiler_params=_params("arbitrary"),
        name="combine_final" if final else "combine",
    )(dest1, dest2, x, route, g.reshape(1, d), ys)


def _plan(route, counts, *, n_total_experts, tm, max_tiles):
    e1 = route[:, 0].astype(jnp.int32)
    e2 = route[:, 1].astype(jnp.int32)
    rank1 = route[:, 4].astype(jnp.int32)
    rank2 = route[:, 5].astype(jnp.int32)
    cnt = counts[0, :n_total_experts].astype(jnp.int32)
    padded = ((cnt + tm - 1) // tm) * tm
    ends = jnp.cumsum(padded)
    offs = ends - padded
    dest1 = offs[e1] + rank1
    dest2 = offs[e2] + rank2
    n_valid = ends[-1] // tm
    tiles = jnp.arange(max_tiles, dtype=jnp.int32)
    t_exp = jnp.minimum(jnp.searchsorted(ends, tiles * tm, side="right").astype(jnp.int32),
                        n_total_experts - 1)
    last = jnp.maximum(n_valid - 1, 0)
    valid = tiles < n_valid
    tile_block = jnp.where(valid, tiles, last).astype(jnp.int32)
    tile_expert = jnp.where(valid, t_exp, t_exp[last]).astype(jnp.int32)
    return dest1, dest2, tile_expert, tile_block, n_valid.reshape(1).astype(jnp.int32)


def kernel(x, norm1_g, w_in, lb_logits, hgrn_norm_g, w_pool, pool_scale, w_branch_a, w_branch_b,
           w_out, norm2_g, w_router_group, w_router_expert, w_gate, w_up, w_down, final_norm_g):
    batch, seq, d = x.shape
    n = batch * seq
    depth = w_in.shape[0]
    width = lb_logits.shape[1]
    pool_width = pool_scale.shape[1]
    n_groups = w_router_group.shape[-1]
    n_experts = w_gate.shape[2]
    ge = n_groups * n_experts
    d_expert = w_gate.shape[-1]
    assert n_groups + ge <= LANES
    col_pool = 4 * width
    col_ga = col_pool + pool_width
    col_gb = col_ga + d

    tm = 256 if n >= 4096 else 32
    max_tiles = (2 * n + ge * (tm - 1) + tm - 1) // tm
    rows = max_tiles * tm

    wg_all = w_gate.reshape(depth * ge, d, d_expert)
    wu_all = w_up.reshape(depth * ge, d, d_expert)
    wd_all = w_down.reshape(depth * ge, d_expert, d)

    xf = x.reshape(n, d)
    h = _rmsnorm(xf, norm1_g[0], BF16)
    for l in range(depth):
        proj = _inproj(h, w_in, l)
        o_a = _hgrn(proj, lb_logits, hgrn_norm_g[l], layer=l, batch=batch, seq=seq, width=width)
        o_b = _pool(proj, w_pool, pool_scale, layer=l, batch=batch, seq=seq, col0=col_pool)
        merged = _merge(o_a, o_b, w_branch_a, w_branch_b, proj, layer=l, col_ga=col_ga, col_gb=col_gb)
        x1 = _outproj(merged, w_out, xf, l)

        w_router = jnp.concatenate([w_router_group[l], w_router_expert[l]], axis=1)
        w_router = jnp.pad(w_router, ((0, 0), (0, LANES - w_router.shape[1])))
        hp, route, counts = _router(x1, norm2_g[l], w_router, n_groups=n_groups, n_experts=n_experts)
        dest1, dest2, tile_expert, tile_block, n_valid = _plan(
            route, counts, n_total_experts=ge, tm=tm, max_tiles=max_tiles)
        xs = _dispatch(hp, dest1, dest2, rows)
        ys = _experts(xs, wg_all, wu_all, wd_all, tile_expert + l * ge, tile_block, n_valid, tm=tm)
        if l + 1 < depth:
            xf, h = _combine(x1, route, norm1_g[l + 1], ys, dest1, dest2, final=False)
        else:
            out = _combine(x1, route, final_norm_g, ys, dest1, dest2, final=True)
    return out.reshape(batch, seq, d)
```

```python
import functools

import jax
import jax.numpy as jnp
from jax import lax
from jax.experimental import pallas as pl
from jax.experimental.pallas import tpu as pltpu

RMS_EPS = 1e-6
MIN_FORGET = 1e-30
LOG2E = 1.4426950408889634
POOL_WINDOWS = (2, 4, 8, 16)
POOL_HALO = 16
HGRN_CHUNK = 64
HGRN_SUB = 16
TOP_K = 2
MOE_SUB_ROWS = 256
MOE_SUBS_PER_SUPER = 4
LANES = 128
VMEM_LIMIT_BYTES = 56 * 1024 * 1024

F32 = jnp.float32
BF16 = jnp.bfloat16


def _params(*sem):
    return pltpu.CompilerParams(dimension_semantics=sem, vmem_limit_bytes=VMEM_LIMIT_BYTES)


def _tile(n, pref):
    t = min(n, pref)
    assert n % t == 0, (n, pref)
    return t


def _sigmoid(z):
    return 1.0 / (1.0 + jnp.exp(-z))


def _dot(a, b):
    return jnp.dot(a, b, preferred_element_type=F32)


def _dot_nt(a, b):
    return lax.dot_general(a, b, (((1,), (1,)), ((), ())), preferred_element_type=F32)


def _rms(x, g):
    return x * lax.rsqrt(jnp.mean(x * x, axis=-1, keepdims=True) + RMS_EPS) * g


def _rmsnorm_kernel(x_ref, g_ref, o_ref):
    o_ref[...] = _rms(x_ref[...], g_ref[...]).astype(o_ref.dtype)


def _rmsnorm(x, g, out_dtype):
    n, d = x.shape
    tr = _tile(n, 256)
    return pl.pallas_call(
        _rmsnorm_kernel,
        out_shape=jax.ShapeDtypeStruct((n, d), out_dtype),
        grid=(n // tr,),
        in_specs=[pl.BlockSpec((tr, d), lambda i: (i, 0)),
                  pl.BlockSpec((1, d), lambda i: (0, 0))],
        out_specs=pl.BlockSpec((tr, d), lambda i: (i, 0)),
        compiler_params=_params("parallel"),
        name="rmsnorm",
    )(x, g.reshape(1, d))


def _inproj_kernel(h_ref, w_ref, o_ref):
    o_ref[...] = _dot(h_ref[...], w_ref[...].astype(BF16))


def _inproj(h, w, layer):
    n, d = h.shape
    cols = w.shape[2]
    bm, bn = _tile(n, 1024), _tile(cols, 512)
    return pl.pallas_call(
        _inproj_kernel,
        out_shape=jax.ShapeDtypeStruct((n, cols), F32),
        grid=(n // bm, cols // bn),
        in_specs=[pl.BlockSpec((bm, d), lambda i, j: (i, 0)),
                  pl.BlockSpec((None, d, bn), lambda i, j: (layer, 0, j))],
        out_specs=pl.BlockSpec((bm, bn), lambda i, j: (i, j)),
        compiler_params=_params("parallel", "arbitrary"),
        name="inproj",
    )(h, w)


def _split3(x):
    hi = x.astype(BF16)
    r1 = x - hi.astype(F32)
    mid = r1.astype(BF16)
    lo = (r1 - mid.astype(F32)).astype(BF16)
    return hi, mid, lo


def _hgrn_kernel(lbl_ref, g_ref, q_ref, f_ref, i_ref, og_ref, o_ref,
                 st_ref, kb_ref, bb_ref, vb_ref, kc_ref, bc_ref, oc_ref, *, layer, n_chunks, n_heads):
    C, SB = HGRN_CHUNK, HGRN_SUB
    dk = g_ref.shape[1]
    nsb = C // SB
    stride = 2 * SB

    @pl.when(pl.program_id(2) == 0)
    def _():
        st_ref[...] = jnp.zeros_like(st_ref)

    zgap = jnp.zeros((SB, dk), F32)
    for hh in range(n_heads):
        for i in range(nsb):
            kb_ref[hh, stride * i:stride * i + SB, :] = zgap
            bb_ref[hh, stride * i:stride * i + SB, :] = zgap
            vb_ref[hh, stride * i:stride * i + SB, :] = zgap

    lbl = lbl_ref[...]
    e = jnp.exp(lbl - jnp.max(lbl, axis=0, keepdims=True))
    p = e / jnp.sum(e, axis=0, keepdims=True)
    lb_all = jnp.sum(p[0:layer + 1], axis=0, keepdims=True) - p[0:1]
    gain = g_ref[...]

    r_i = lax.broadcasted_iota(jnp.int32, (C, C), 0)
    c_i = lax.broadcasted_iota(jnp.int32, (C, C), 1)
    tri = jnp.where(r_i >= c_i, 1.0, 0.0).astype(BF16)


    def gates(hh, r0):
        rows, cols = pl.ds(r0, C), slice(hh * dk, (hh + 1) * dk)
        lb = lb_all[:, cols]
        z = f_ref[rows, cols]
        f = lb + (1.0 - lb) * _sigmoid(z)
        lf = jnp.log(jnp.maximum(f, MIN_FORGET))
        k = (1.0 - lb) * _sigmoid(-z)
        hi, mid, lo = _split3(lf)
        b2 = (_dot(tri, hi) + _dot(tri, mid) + _dot(tri, lo)) * LOG2E
        v = i_ref[rows, cols]
        kc_ref[hh] = k
        bc_ref[hh] = b2
        for i in range(nsb):
            d0 = stride * i + SB
            kb_ref[hh, d0:d0 + SB, :] = k[SB * i:SB * (i + 1)]
            bb_ref[hh, d0:d0 + SB, :] = b2[SB * i:SB * (i + 1)]
            vb_ref[hh, d0:d0 + SB, :] = v[SB * i:SB * (i + 1)]

    def earlier_blocks(hh, r0):
        rows, cols = pl.ds(r0, C), slice(hh * dk, (hh + 1) * dk)
        q, v = q_ref[rows, cols], i_ref[rows, cols]
        k, b2 = kc_ref[hh], bc_ref[hh]
        o_inter = _dot_nt((q * jnp.exp2(b2)).astype(BF16), st_ref[hh].astype(BF16))
        pieces = [jnp.zeros((SB, dk), F32)]
        for i in range(1, nsb):
            lo_r, hi_r = i * SB, (i + 1) * SB
            ref = b2[lo_r - 1:lo_r, :]
            qt = q[lo_r:hi_r] * jnp.exp2(jnp.minimum(b2[lo_r:hi_r] - ref, 0.0))
            kt = k[:lo_r] * jnp.exp2(jnp.minimum(ref - b2[:lo_r], 0.0))
            a = _dot_nt(qt.astype(BF16), kt.astype(BF16))
            pieces.append(_dot(a.astype(BF16), v[:lo_r].astype(BF16)))
        oc_ref[hh] = o_inter + jnp.concatenate(pieces, axis=0)

    def same_block(hh, r0):
        cols = slice(hh * dk, (hh + 1) * dk)
        for i in range(nsb):
            for r in range(SB // 8):
                t0 = SB * i + 8 * r
                base = stride * i + SB + 8 * r
                q8 = q_ref[pl.ds(r0 + t0, 8), cols]
                b8 = bb_ref[hh, base:base + 8, :]
                acc = oc_ref[hh, t0:t0 + 8, :]
                for j in range(8 * r + 8):
                    kj = kb_ref[hh, base - j:base - j + 8, :]
                    bj = bb_ref[hh, base - j:base - j + 8, :]
                    vj = vb_ref[hh, base - j:base - j + 8, :]
                    aj = jnp.sum(q8 * kj * jnp.exp2(jnp.minimum(b8 - bj, 0.0)), axis=-1, keepdims=True)
                    acc = acc + aj * vj
                oc_ref[hh, t0:t0 + 8, :] = acc

    def finish(hh, r0):
        rows, cols = pl.ds(r0, C), slice(hh * dk, (hh + 1) * dk)
        k, b2, v = kc_ref[hh], bc_ref[hh], i_ref[rows, cols]
        b_last = b2[C - 1:C, :]
        khat = k * jnp.exp2(b_last - b2)
        st_ref[hh] = st_ref[hh] * jnp.exp2(b_last) + _dot(v.T.astype(BF16), khat.astype(BF16))
        og = og_ref[rows, cols]
        y = _rms(oc_ref[hh], gain) * (og * _sigmoid(og))
        o_ref[rows, cols] = y.astype(o_ref.dtype)

    def chunk(c, carry):
        r0 = pl.multiple_of(c * C, C)
        for phase in (gates, earlier_blocks, same_block, finish):
            for hh in range(n_heads):
                phase(hh, r0)
        return carry

    lax.fori_loop(0, n_chunks, chunk, 0)


def _hgrn(proj, lb_logits, gain, *, layer, batch, seq, width):
    n = proj.shape[0]
    dk = gain.shape[-1]
    heads = width // dk
    hb = _tile(heads, 4)
    hg = heads // hb
    t = _tile(seq, 512)
    ns = seq // t
    n_layers = lb_logits.shape[0]

    def col(seg):
        return pl.BlockSpec((t, hb * dk), lambda b, h, s, seg=seg: (b * ns + s, seg * hg + h))

    kern = functools.partial(_hgrn_kernel, layer=layer, n_chunks=t // HGRN_CHUNK, n_heads=hb)
    buf = pltpu.VMEM((hb, 2 * HGRN_CHUNK, dk), F32)
    cbuf = pltpu.VMEM((hb, HGRN_CHUNK, dk), F32)
    return pl.pallas_call(
        kern,
        out_shape=jax.ShapeDtypeStruct((n, width), BF16),
        grid=(batch, hg, ns),
        in_specs=[pl.BlockSpec((n_layers, hb * dk), lambda b, h, s: (0, h)),
                  pl.BlockSpec((1, dk), lambda b, h, s: (0, 0)),
                  col(0), col(1), col(2), col(3)],
        out_specs=pl.BlockSpec((t, hb * dk), lambda b, h, s: (b * ns + s, h)),
        scratch_shapes=[pltpu.VMEM((hb, dk, dk), F32), buf, buf, buf, cbuf, cbuf, cbuf],
        compiler_params=_params("parallel", "parallel", "arbitrary"),
        name="hgrn2",
    )(lb_logits, gain.reshape(1, dk), proj, proj, proj, proj)


def _pool_kernel(u_ref, halo_ref, w_ref, sc_ref, o_ref, *, windows):
    g = pl.program_id(2)
    s = pl.program_id(1)
    t = u_ref.shape[0]
    u = u_ref[...]
    win = jnp.int32(windows[0])
    for gi in range(1, len(windows)):
        win = jnp.where(g == gi, jnp.int32(windows[gi]), win)
    r_i = lax.broadcasted_iota(jnp.int32, (t, POOL_HALO + t), 0)
    c_i = lax.broadcasted_iota(jnp.int32, (t, POOL_HALO + t), 1) - POOL_HALO
    lag = r_i - c_i
    inside = (lag >= 0) & (lag < win) & (c_i + s * t >= 0)
    band = jnp.where(inside, 1.0, 0.0).astype(BF16)
    ext = jnp.concatenate([halo_ref[...].astype(BF16), u.astype(BF16)], axis=0)
    wsum = _dot(band, ext)
    pos = (lax.broadcasted_iota(jnp.int32, (t, 1), 0) + s * t + 1).astype(F32)
    pooled = wsum / jnp.minimum(pos, win.astype(F32)) - u
    mixed = _dot(pooled.astype(BF16), w_ref[...].astype(BF16))
    o_ref[...] = (mixed * sc_ref[...]).astype(o_ref.dtype)


def _pool(proj, w_pool, pool_scale, *, layer, batch, seq, col0):
    n = proj.shape[0]
    _, groups, cg, _ = w_pool.shape
    t = _tile(seq, 512)
    ns = seq // t
    hb = t // POOL_HALO
    assert col0 % cg == 0 and groups == len(POOL_WINDOWS)
    cb = col0 // cg
    kern = functools.partial(_pool_kernel, windows=POOL_WINDOWS)
    return pl.pallas_call(
        kern,
        out_shape=jax.ShapeDtypeStruct((n, groups * cg), BF16),
        grid=(batch, ns, groups),
        in_specs=[pl.BlockSpec((t, cg), lambda b, s, g: (b * ns + s, cb + g)),
                  pl.BlockSpec((POOL_HALO, cg),
                               lambda b, s, g: (jnp.maximum((b * ns + s) * hb - 1, 0), cb + g)),
                  pl.BlockSpec((None, None, cg, cg), lambda b, s, g: (layer, g, 0, 0)),
                  pl.BlockSpec((1, cg), lambda b, s, g: (0, g))],
        out_specs=pl.BlockSpec((t, cg), lambda b, s, g: (b * ns + s, g)),
        compiler_params=_params("parallel", "parallel", "arbitrary"),
        name="pool_mixer",
    )(proj, proj, w_pool, pool_scale[layer].reshape(1, groups * cg))


def _merge_kernel(oa_ref, ob_ref, wa_ref, wb_ref, ga_ref, gb_ref, o_ref):
    a = _dot(oa_ref[...], wa_ref[...].astype(BF16))
    b = _dot(ob_ref[...], wb_ref[...].astype(BF16))
    o_ref[...] = (_sigmoid(ga_ref[...]) * a + _sigmoid(gb_ref[...]) * b).astype(o_ref.dtype)


def _merge(o_a, o_b, w_a, w_b, proj, *, layer, col_ga, col_gb):
    n, wa = o_a.shape
    wb = o_b.shape[1]
    d = w_a.shape[2]
    bm, bn = _tile(n, 1024), _tile(d, 512)
    assert col_ga % bn == 0 and col_gb % bn == 0
    ca, cb = col_ga // bn, col_gb // bn
    return pl.pallas_call(
        _merge_kernel,
        out_shape=jax.ShapeDtypeStruct((n, d), BF16),
        grid=(n // bm, d // bn),
        in_specs=[pl.BlockSpec((bm, wa), lambda i, j: (i, 0)),
                  pl.BlockSpec((bm, wb), lambda i, j: (i, 0)),
                  pl.BlockSpec((None, wa, bn), lambda i, j: (layer, 0, j)),
                  pl.BlockSpec((None, wb, bn), lambda i, j: (layer, 0, j)),
                  pl.BlockSpec((bm, bn), lambda i, j: (i, ca + j)),
                  pl.BlockSpec((bm, bn), lambda i, j: (i, cb + j))],
        out_specs=pl.BlockSpec((bm, bn), lambda i, j: (i, j)),
        compiler_params=_params("parallel", "arbitrary"),
        name="gated_merge",
    )(o_a, o_b, w_a, w_b, proj, proj)


def _outproj_kernel(m_ref, w_ref, x_ref, o_ref):
    o_ref[...] = x_ref[...] + _dot(m_ref[...], w_ref[...].astype(BF16))


def _outproj(merged, w, x, layer):
    n, d = merged.shape
    dn = w.shape[2]
    bm, bn = _tile(n, 1024), _tile(dn, 512)
    return pl.pallas_call(
        _outproj_kernel,
        out_shape=jax.ShapeDtypeStruct((n, dn), F32),
        grid=(n // bm, dn // bn),
        in_specs=[pl.BlockSpec((bm, d), lambda i, j: (i, 0)),
                  pl.BlockSpec((None, d, bn), lambda i, j: (layer, 0, j)),
                  pl.BlockSpec((bm, bn), lambda i, j: (i, j))],
        out_specs=pl.BlockSpec((bm, bn), lambda i, j: (i, j)),
        compiler_params=_params("parallel", "arbitrary"),
        name="outproj_residual",
    )(merged, w, x)


def _router_kernel(x_ref, g_ref, wr_ref, hp_ref, route_ref, cnt_ref, run_ref, *, n_groups, n_experts):
    tr, d = x_ref.shape
    half = d // 2

    @pl.when(pl.program_id(0) == 0)
    def _():
        run_ref[...] = jnp.zeros_like(run_ref)

    h = _rms(x_ref[...], g_ref[...])
    hb = h.astype(BF16)
    hp_ref[...] = _pack_halves(h[:, :half], h[:, half:])

    h_lo = (h - hb.astype(F32)).astype(BF16)
    wr = wr_ref[...]
    w_hi = wr.astype(BF16)
    w_lo = (wr - w_hi.astype(F32)).astype(BF16)
    logits = _dot(hb, w_hi) + _dot(hb, w_lo) + _dot(h_lo, w_hi)

    lane = lax.broadcasted_iota(jnp.int32, (tr, LANES), 1).astype(F32)
    ninf = jnp.float32(-jnp.inf)
    big = jnp.float32(1e9)
    is_g = lane < n_groups
    gl = jnp.where(is_g, logits, ninf)
    gmax = jnp.max(gl, axis=-1, keepdims=True)
    gidx = jnp.min(jnp.where(gl == gmax, lane, big), axis=-1, keepdims=True)
    g_w = 1.0 / jnp.sum(jnp.where(is_g, jnp.exp(gl - gmax), 0.0), axis=-1, keepdims=True)

    e_lo = n_groups + gidx * n_experts
    el = jnp.where(lane >= e_lo, jnp.where(lane < e_lo + n_experts, logits, ninf), ninf)
    v1 = jnp.max(el, axis=-1, keepdims=True)
    i1 = jnp.min(jnp.where(el == v1, lane, big), axis=-1, keepdims=True)
    el2 = jnp.where(lane == i1, ninf, el)
    v2 = jnp.max(el2, axis=-1, keepdims=True)
    i2 = jnp.min(jnp.where(el2 == v2, lane, big), axis=-1, keepdims=True)
    tt = jnp.exp(v2 - v1)
    w1 = g_w / (1.0 + tt)
    w2 = g_w * tt / (1.0 + tt)
    e1 = i1 - n_groups
    e2 = i2 - n_groups

    oh1 = lane == e1
    oh2 = lane == e2
    cnt = jnp.where(oh1, 1.0, 0.0) + jnp.where(oh2, 1.0, 0.0)
    r_i = lax.broadcasted_iota(jnp.int32, (tr, tr), 0)
    c_i = lax.broadcasted_iota(jnp.int32, (tr, tr), 1)
    strict = jnp.where(r_i > c_i, 1.0, 0.0).astype(BF16)
    before = _dot(strict, cnt.astype(BF16)) + run_ref[0:1, :]
    rank1 = jnp.sum(jnp.where(oh1, before, 0.0), axis=-1, keepdims=True)
    rank2 = jnp.sum(jnp.where(oh2, before, 0.0), axis=-1, keepdims=True)
    run = run_ref[0:1, :] + jnp.sum(cnt, axis=0, keepdims=True)
    run_ref[...] = jnp.broadcast_to(run, run_ref.shape)
    cnt_ref[...] = jnp.broadcast_to(run, cnt_ref.shape)

    out = jnp.zeros((tr, LANES), F32)
    for idx, val in enumerate((e1, e2, w1, w2, rank1, rank2)):
        out = jnp.where(lane == idx, val, out)
    route_ref[...] = out


def _router(x, g, w_router, *, n_groups, n_experts):
    n, d = x.shape
    tr = _tile(n, 256)
    kern = functools.partial(_router_kernel, n_groups=n_groups, n_experts=n_experts)
    return pl.pallas_call(
        kern,
        out_shape=(jax.ShapeDtypeStruct((n, d // 2), jnp.uint32),
                   jax.ShapeDtypeStruct((n, LANES), F32),
                   jax.ShapeDtypeStruct((8, LANES), F32)),
        grid=(n // tr,),
        in_specs=[pl.BlockSpec((tr, d), lambda i: (i, 0)),
                  pl.BlockSpec((1, d), lambda i: (0, 0)),
                  pl.BlockSpec((d, LANES), lambda i: (0, 0))],
        out_specs=(pl.BlockSpec((tr, d // 2), lambda i: (i, 0)),
                   pl.BlockSpec((tr, LANES), lambda i: (i, 0)),
                   pl.BlockSpec((8, LANES), lambda i: (0, 0))),
        scratch_shapes=[pltpu.VMEM((8, LANES), F32)],
        compiler_params=_params("arbitrary"),
        name="norm_router",
    )(x, g.reshape(1, d), w_router)


def _row_copy(src_ref, src_row, dst_ref, dst_row, sem):
    return pltpu.make_async_copy(src_ref.at[pl.ds(src_row, 1)], dst_ref.at[pl.ds(dst_row, 1)], sem)


def _dispatch_kernel(d1_ref, d2_ref, zb_ref, zf_ref, h_ref, xs_ref, zero_ref, sem, *, n_exp):
    tr = h_ref.shape[0]
    sub = zero_ref.shape[0]
    base = pl.program_id(0) * tr

    def zero_copy(e):
        dst = xs_ref.at[pl.ds(pl.multiple_of(zb_ref[e] * sub, sub), sub)]
        return pltpu.make_async_copy(zero_ref, dst, sem.at[2])

    @pl.when(pl.program_id(0) == 0)
    def _():
        zero_ref[...] = jnp.zeros_like(zero_ref)

        def z_start(e, carry):
            @pl.when(zf_ref[e] != 0)
            def _():
                zero_copy(e).start()
            return carry

        def z_wait(e, carry):
            @pl.when(zf_ref[e] != 0)
            def _():
                zero_copy(e).wait()
            return carry

        lax.fori_loop(0, n_exp, z_start, 0)
        lax.fori_loop(0, n_exp, z_wait, 0)

    def issue(r, carry):
        _row_copy(h_ref, r, xs_ref, d1_ref[base + r], sem.at[0]).start()
        _row_copy(h_ref, r, xs_ref, d2_ref[base + r], sem.at[1]).start()
        return carry

    lax.fori_loop(0, tr, issue, 0)

    def drain(r, carry):
        _row_copy(h_ref, r, xs_ref, d1_ref[base + r], sem.at[0]).wait()
        _row_copy(h_ref, r, xs_ref, d2_ref[base + r], sem.at[1]).wait()
        return carry

    lax.fori_loop(0, tr, drain, 0)


def _dispatch(hp, dest1, dest2, zero_block, zero_flag, *, rows, sub):
    n, dw = hp.shape
    tr = _tile(n, 256)
    grid_spec = pltpu.PrefetchScalarGridSpec(
        num_scalar_prefetch=4,
        grid=(n // tr,),
        in_specs=[pl.BlockSpec((tr, dw), lambda i, *_: (i, 0))],
        out_specs=pl.BlockSpec(memory_space=pl.ANY),
        scratch_shapes=[pltpu.VMEM((sub, dw), hp.dtype), pltpu.SemaphoreType.DMA((3,))],
    )
    return pl.pallas_call(
        functools.partial(_dispatch_kernel, n_exp=zero_block.shape[0]),
        out_shape=jax.ShapeDtypeStruct((rows, dw), hp.dtype),
        grid_spec=grid_spec,
        compiler_params=_params("arbitrary"),
        name="dispatch",
    )(dest1, dest2, zero_block, zero_flag, hp)


def _pack_halves(lo, hi):
    lo_w = pltpu.bitcast(lo.astype(BF16).astype(F32), jnp.uint32) >> 16
    hi_w = pltpu.bitcast(hi.astype(BF16).astype(F32), jnp.uint32) & jnp.uint32(0xFFFF0000)
    return lo_w | hi_w


def _unpack_halves(w):
    return (pltpu.bitcast(w << 16, F32), pltpu.bitcast(w & jnp.uint32(0xFFFF0000), F32))


def _expert_kernel(se_ref, sb_ref, sn_ref, ns_ref, xs_ref, wg_ref, wu_ref, wdl_ref, wdh_ref, o_ref,
                   hid_ref, *, n_up, sub):
    del se_ref, sb_ref
    s, p = pl.program_id(0), pl.program_id(1)
    valid = s < ns_ref[0]
    n_sub = sn_ref[s]
    half = xs_ref.shape[1]
    fc = wg_ref.shape[1]

    @pl.when(jnp.logical_and(valid, p < n_up))
    def _():
        wg = wg_ref[...].astype(BF16)
        wu = wu_ref[...].astype(BF16)
        for q in range(xs_ref.shape[0] // sub):
            rows = slice(q * sub, (q + 1) * sub)

            @pl.when(q < n_sub)
            def _():
                x_lo, x_hi = _unpack_halves(xs_ref[rows, :])
                x_lo, x_hi = x_lo.astype(BF16), x_hi.astype(BF16)
                a = _dot(x_lo, wg[:half]) + _dot(x_hi, wg[half:])
                b = _dot(x_lo, wu[:half]) + _dot(x_hi, wu[half:])
                hid_ref[p, rows, :] = (a * _sigmoid(a) * b).astype(BF16)

    @pl.when(jnp.logical_and(valid, p >= n_up))
    def _():
        wl = wdl_ref[...].astype(BF16)
        wh = wdh_ref[...].astype(BF16)
        for q in range(xs_ref.shape[0] // sub):
            rows = slice(q * sub, (q + 1) * sub)

            @pl.when(q < n_sub)
            def _():
                y_lo = _dot(hid_ref[0, rows, :], wl[0:fc])
                y_hi = _dot(hid_ref[0, rows, :], wh[0:fc])
                for j in range(1, n_up):
                    y_lo += _dot(hid_ref[j, rows, :], wl[j * fc:(j + 1) * fc])
                    y_hi += _dot(hid_ref[j, rows, :], wh[j * fc:(j + 1) * fc])
                o_ref[rows, :] = _pack_halves(y_lo, y_hi)

            @pl.when(q >= n_sub)
            def _():
                o_ref[rows, :] = jnp.zeros((sub, o_ref.shape[1]), o_ref.dtype)


def _experts(xs, w_gate, w_up, w_down, sup_expert, sup_block, sup_nsub, n_super, *, sub, sup):
    rows, dw = xs.shape
    _, d, f = w_gate.shape
    fc = _tile(f, 256)
    n_up = f // fc
    dc = _tile(dw, 512)
    n_down = dw // dc
    n_phase = n_up + n_down

    def phase(s, p, ns):
        return jnp.where(s < ns[0], p, n_phase - 1)

    def up_c(s, p, ns):
        return jnp.minimum(phase(s, p, ns), n_up - 1)

    def down_c(s, p, ns):
        return jnp.maximum(phase(s, p, ns) - n_up, 0)

    grid_spec = pltpu.PrefetchScalarGridSpec(
        num_scalar_prefetch=4,
        grid=(rows // sup, n_phase),
        in_specs=[pl.BlockSpec((sup, dw), lambda s, p, se, sb, sn, ns: (sb[s], 0)),
                  pl.BlockSpec((None, d, fc), lambda s, p, se, sb, sn, ns: (se[s], 0, up_c(s, p, ns))),
                  pl.BlockSpec((None, d, fc), lambda s, p, se, sb, sn, ns: (se[s], 0, up_c(s, p, ns))),
                  pl.BlockSpec((None, f, dc), lambda s, p, se, sb, sn, ns: (se[s], 0, down_c(s, p, ns))),
                  pl.BlockSpec((None, f, dc),
                               lambda s, p, se, sb, sn, ns: (se[s], 0, n_down + down_c(s, p, ns)))],
        out_specs=pl.BlockSpec((sup, dc), lambda s, p, se, sb, sn, ns: (sb[s], down_c(s, p, ns))),
        scratch_shapes=[pltpu.VMEM((n_up, sup, fc), BF16)],
    )
    return pl.pallas_call(
        functools.partial(_expert_kernel, n_up=n_up, sub=sub),
        out_shape=jax.ShapeDtypeStruct((rows, dw), jnp.uint32),
        grid_spec=grid_spec,
        compiler_params=_params("arbitrary", "arbitrary"),
        name="expert_mlp",
    )(sup_expert, sup_block, sup_nsub, n_super, xs, w_gate, w_up, w_down, w_down)


def _combine_kernel(d1_ref, d2_ref, x_ref, route_ref, g_ref, ys_ref, *rest, final):
    if final:
        y_ref, buf1, buf2, sem = rest
    else:
        x2_ref, hn_ref, buf1, buf2, sem = rest
    tr = x_ref.shape[0]
    base = pl.program_id(0) * tr

    def issue(r, carry):
        _row_copy(ys_ref, d1_ref[base + r], buf1, r, sem.at[0]).start()
        _row_copy(ys_ref, d2_ref[base + r], buf2, r, sem.at[1]).start()
        return carry

    lax.fori_loop(0, tr, issue, 0)

    def drain(r, carry):
        _row_copy(ys_ref, d1_ref[base + r], buf1, r, sem.at[0]).wait()
        _row_copy(ys_ref, d2_ref[base + r], buf2, r, sem.at[1]).wait()
        return carry

    lax.fori_loop(0, tr, drain, 0)

    route = route_ref[...]
    w1, w2 = route[:, 2:3], route[:, 3:4]
    half = buf1.shape[1]
    d = 2 * half
    lo1, hi1 = _unpack_halves(buf1[...])
    lo2, hi2 = _unpack_halves(buf2[...])
    x2_lo = x_ref[:, :half] + w1 * lo1 + w2 * lo2
    x2_hi = x_ref[:, half:] + w1 * hi1 + w2 * hi2
    ms = (jnp.sum(x2_lo * x2_lo, axis=-1, keepdims=True) +
          jnp.sum(x2_hi * x2_hi, axis=-1, keepdims=True)) * (1.0 / d)
    inv = lax.rsqrt(ms + RMS_EPS)
    out_ref = y_ref if final else hn_ref
    if not final:
        x2_ref[:, :half] = x2_lo
        x2_ref[:, half:] = x2_hi
    out_ref[:, :half] = (x2_lo * inv * g_ref[:, :half]).astype(out_ref.dtype)
    out_ref[:, half:] = (x2_hi * inv * g_ref[:, half:]).astype(out_ref.dtype)


def _combine(x, route, g, ys, dest1, dest2, *, final):
    n, d = x.shape
    tr = _tile(n, 256)
    row = lambda i, d1, d2: (i, 0)
    if final:
        out_shape = jax.ShapeDtypeStruct((n, d), F32)
        out_specs = pl.BlockSpec((tr, d), row)
    else:
        out_shape = (jax.ShapeDtypeStruct((n, d), F32), jax.ShapeDtypeStruct((n, d), BF16))
        out_specs = (pl.BlockSpec((tr, d), row), pl.BlockSpec((tr, d), row))
    grid_spec = pltpu.PrefetchScalarGridSpec(
        num_scalar_prefetch=2,
        grid=(n // tr,),
        in_specs=[pl.BlockSpec((tr, d), row),
                  pl.BlockSpec((tr, LANES), row),
                  pl.BlockSpec((1, d), lambda i, d1, d2: (0, 0)),
                  pl.BlockSpec(memory_space=pl.ANY)],
        out_specs=out_specs,
        scratch_shapes=[pltpu.VMEM((tr, d // 2), jnp.uint32), pltpu.VMEM((tr, d // 2), jnp.uint32),
                        pltpu.SemaphoreType.DMA((2,))],
    )
    return pl.pallas_call(
        functools.partial(_combine_kernel, final=final),
        out_shape=out_shape,
        grid_spec=grid_spec,
        compiler_params=_params("arbitrary"),
        name="combine_final" if final else "combine",
    )(dest1, dest2, x, route, g.reshape(1, d), ys)


def _plan(route, counts, *, ge, sub, sup, max_super):
    i32 = jnp.int32
    e1, e2 = route[:, 0].astype(i32), route[:, 1].astype(i32)
    rank1, rank2 = route[:, 4].astype(i32), route[:, 5].astype(i32)
    cnt = counts[0, :ge].astype(i32)
    n_sub = (cnt + sub - 1) // sub
    n_sup = (cnt + sup - 1) // sup
    ends = jnp.cumsum(n_sup * sup)
    offs = ends - n_sup * sup
    dest1 = offs[e1] + rank1
    dest2 = offs[e2] + rank2
    sup_ends = jnp.cumsum(n_sup)
    n_super = sup_ends[-1]
    s = jnp.arange(max_super, dtype=i32)
    se = jnp.minimum(jnp.sum((sup_ends[None, :] <= s[:, None]).astype(i32), axis=1), ge - 1)
    k = s - (sup_ends[se] - n_sup[se])
    sb = offs[se] // sup + k
    sn = jnp.clip(n_sub[se] - k * (sup // sub), 0, sup // sub)
    last = jnp.maximum(n_super - 1, 0)
    valid = s < n_super
    sup_expert = jnp.where(valid, se, se[last]).astype(i32)
    sup_block = jnp.where(valid, sb, sb[last]).astype(i32)
    sup_nsub = jnp.where(valid, sn, 0).astype(i32)
    zero_block = (offs // sub + cnt // sub).astype(i32)
    zero_flag = (cnt % sub != 0).astype(i32)
    return (dest1, dest2, sup_expert, sup_block, sup_nsub, n_super.reshape(1).astype(i32),
            zero_block, zero_flag)


def kernel(x, norm1_g, w_in, lb_logits, hgrn_norm_g, w_pool, pool_scale, w_branch_a, w_branch_b,
           w_out, norm2_g, w_router_group, w_router_expert, w_gate, w_up, w_down, final_norm_g):
    batch, seq, d = x.shape
    n = batch * seq
    depth = w_in.shape[0]
    width = lb_logits.shape[1]
    pool_width = pool_scale.shape[1]
    n_groups = w_router_group.shape[-1]
    n_experts = w_gate.shape[2]
    ge = n_groups * n_experts
    d_expert = w_gate.shape[-1]
    assert n_groups + ge <= LANES
    col_pool = 4 * width
    col_ga = col_pool + pool_width
    col_gb = col_ga + d

    sub = MOE_SUB_ROWS if n >= 4096 else 32
    sup = MOE_SUBS_PER_SUPER * sub
    max_super = (TOP_K * n + ge * (sup - 1)) // sup
    rows = max_super * sup

    wg_all = w_gate.reshape(depth * ge, d, d_expert)
    wu_all = w_up.reshape(depth * ge, d, d_expert)
    wd_all = w_down.reshape(depth * ge, d_expert, d)

    xf = x.reshape(n, d)
    h = _rmsnorm(xf, norm1_g[0], BF16)
    for l in range(depth):
        proj = _inproj(h, w_in, l)
        o_a = _hgrn(proj, lb_logits, hgrn_norm_g[l], layer=l, batch=batch, seq=seq, width=width)
        o_b = _pool(proj, w_pool, pool_scale, layer=l, batch=batch, seq=seq, col0=col_pool)
        merged = _merge(o_a, o_b, w_branch_a, w_branch_b, proj, layer=l, col_ga=col_ga, col_gb=col_gb)
        x1 = _outproj(merged, w_out, xf, l)

        w_router = jnp.concatenate([w_router_group[l], w_router_expert[l]], axis=1)
        w_router = jnp.pad(w_router, ((0, 0), (0, LANES - w_router.shape[1])))
        hp, route, counts = _router(x1, norm2_g[l], w_router, n_groups=n_groups, n_experts=n_experts)
        (dest1, dest2, sup_expert, sup_block, sup_nsub, n_super, zero_block, zero_flag) = _plan(
            route, counts, ge=ge, sub=sub, sup=sup, max_super=max_super)
        xs = _dispatch(hp, dest1, dest2, zero_block, zero_flag, rows=rows, sub=sub)
        ys = _experts(xs, wg_all, wu_all, wd_all, sup_expert + l * ge, sup_block, sup_nsub, n_super,
                      sub=sub, sup=sup)
        if l + 1 < depth:
            xf, h = _combine(x1, route, norm1_g[l + 1], ys, dest1, dest2, final=False)
        else:
            out = _combine(x1, route, final_norm_g, ys, dest1, dest2, final=True)
    return out.reshape(batch, seq, d)
```

```python
import functools

import jax
import jax.numpy as jnp
from jax import lax
from jax.experimental import pallas as pl
from jax.experimental.pallas import tpu as pltpu

RMS_EPS = 1e-6
MIN_FORGET = 1e-30
LOG2E = 1.4426950408889634
POOL_WINDOWS = (2, 4, 8, 16)
POOL_HALO = 16
HGRN_CHUNK = 64
HGRN_PAD = 8
HGRN_LEVEL_BLOCKS = (64, 32, 16, 8, 4)
TOP_K = 2
MOE_SUB_ROWS = 256
MOE_SUBS_PER_SUPER = 4
LANES = 128
ROUTE_COLS = 8
VMEM_LIMIT_BYTES = 56 * 1024 * 1024

F32 = jnp.float32
BF16 = jnp.bfloat16


def _params(*sem):
    return pltpu.CompilerParams(dimension_semantics=sem, vmem_limit_bytes=VMEM_LIMIT_BYTES)


def _tile(n, pref):
    t = min(n, pref)
    assert n % t == 0, (n, pref)
    return t


def _sigmoid(z):
    return 1.0 / (1.0 + jnp.exp(-z))


def _dot(a, b):
    return jnp.dot(a, b, preferred_element_type=F32)


def _dot_nt(a, b):
    return lax.dot_general(a, b, (((1,), (1,)), ((), ())), preferred_element_type=F32)


def _rms(x, g):
    return x * lax.rsqrt(jnp.mean(x * x, axis=-1, keepdims=True) + RMS_EPS) * g


def _rmsnorm_kernel(x_ref, g_ref, o_ref):
    o_ref[...] = _rms(x_ref[...], g_ref[...]).astype(o_ref.dtype)


def _rmsnorm(x, g, out_dtype):
    n, d = x.shape
    tr = _tile(n, 256)
    return pl.pallas_call(
        _rmsnorm_kernel,
        out_shape=jax.ShapeDtypeStruct((n, d), out_dtype),
        grid=(n // tr,),
        in_specs=[pl.BlockSpec((tr, d), lambda i: (i, 0)),
                  pl.BlockSpec((1, d), lambda i: (0, 0))],
        out_specs=pl.BlockSpec((tr, d), lambda i: (i, 0)),
        compiler_params=_params("parallel"),
        name="rmsnorm",
    )(x, g.reshape(1, d))


def _inproj_kernel(h_ref, w_ref, o_ref):
    o_ref[...] = _dot(h_ref[...], w_ref[...].astype(BF16))


def _inproj(h, w, layer):
    n, d = h.shape
    cols = w.shape[2]
    bm, bn = _tile(n, 1024), _tile(cols, 512)
    return pl.pallas_call(
        _inproj_kernel,
        out_shape=jax.ShapeDtypeStruct((n, cols), F32),
        grid=(n // bm, cols // bn),
        in_specs=[pl.BlockSpec((bm, d), lambda i, j: (i, 0)),
                  pl.BlockSpec((None, d, bn), lambda i, j: (layer, 0, j))],
        out_specs=pl.BlockSpec((bm, bn), lambda i, j: (i, j)),
        compiler_params=_params("parallel", "arbitrary"),
        name="inproj",
    )(h, w)


def _split3(x):
    hi = x.astype(BF16)
    r1 = x - hi.astype(F32)
    mid = r1.astype(BF16)
    lo = (r1 - mid.astype(F32)).astype(BF16)
    return hi, mid, lo


def _hgrn_kernel(lbl_ref, g_ref, q_ref, f_ref, i_ref, og_ref, o_ref,
                 st_ref, kp_ref, bp_ref, vp_ref, oc_ref, lm_ref, *, layer, n_chunks, n_heads):
    C, PAD = HGRN_CHUNK, HGRN_PAD
    LEVEL_BLOCKS = HGRN_LEVEL_BLOCKS
    dk = g_ref.shape[1]

    @pl.when(pl.program_id(2) == 0)
    def _():
        st_ref[...] = jnp.zeros_like(st_ref)

    zpad = jnp.zeros((PAD, dk), F32)
    for hh in range(n_heads):
        kp_ref[hh, 0:PAD, :] = zpad
        bp_ref[hh, 0:PAD, :] = zpad
        vp_ref[hh, 0:PAD, :] = zpad

    t_i = lax.broadcasted_iota(jnp.int32, (C, C), 0)
    s_i = lax.broadcasted_iota(jnp.int32, (C, C), 1)
    for lvl, blk in enumerate(LEVEL_BLOCKS):
        same = (t_i // blk) == (s_i // blk)
        take = same & ((t_i % blk) >= blk // 2) & ((s_i % blk) < blk // 2)
        lm_ref[lvl] = jnp.where(take, 1.0, 0.0)
    sub8 = lax.broadcasted_iota(jnp.int32, (8, dk), 0)

    lbl = lbl_ref[...]
    e = jnp.exp(lbl - jnp.max(lbl, axis=0, keepdims=True))
    p = e / jnp.sum(e, axis=0, keepdims=True)
    lb_all = jnp.sum(p[0:layer + 1], axis=0, keepdims=True) - p[0:1]
    gain = g_ref[...]

    tri = jnp.where(t_i >= s_i, 1.0, 0.0).astype(BF16)
    tri3 = jnp.concatenate([tri, tri, tri], axis=1)


    def gates(hh, r0):
        rows, cols = pl.ds(r0, C), slice(hh * dk, (hh + 1) * dk)
        lb = lb_all[:, cols]
        z = f_ref[rows, cols]
        f = lb + (1.0 - lb) * _sigmoid(z)
        lf = jnp.log(jnp.maximum(f, MIN_FORGET))
        k = (1.0 - lb) * _sigmoid(-z)
        hi, mid, lo = _split3(lf)
        b2 = _dot(tri3, jnp.concatenate([hi, mid, lo], axis=0)) * LOG2E
        kp_ref[hh, PAD:PAD + C, :] = k
        bp_ref[hh, PAD:PAD + C, :] = b2
        vp_ref[hh, PAD:PAD + C, :] = i_ref[rows, cols]

    def scores(hh, r0):
        rows, cols = pl.ds(r0, C), slice(hh * dk, (hh + 1) * dk)
        q, v = q_ref[rows, cols], i_ref[rows, cols]
        k, b2 = kp_ref[hh, PAD:PAD + C, :], bp_ref[hh, PAD:PAD + C, :]
        o = _dot_nt((q * jnp.exp2(b2)).astype(BF16), st_ref[hh].astype(BF16))

        a = jnp.zeros((C, C), F32)
        for lvl, blk in enumerate(LEVEL_BLOCKS):
            hb_ = blk // 2
            if hb_ >= 8:
                xs_, es_ = [], []
                for b0 in range(0, C, blk):
                    m = bp_ref[hh, PAD + b0 + hb_ - 1:PAD + b0 + hb_, :]
                    xs_ += [k[b0:b0 + hb_], q[b0 + hb_:b0 + blk]]
                    es_ += [m - b2[b0:b0 + hb_], b2[b0 + hb_:b0 + blk] - m]
                x = jnp.concatenate(xs_, axis=0)
                e = jnp.minimum(jnp.concatenate(es_, axis=0), 0.0)
            else:
                up8 = (sub8 & (blk - 1)) >= hb_
                xs_, es_ = [], []
                for g0 in range(0, C, 8):
                    rows_m = [bp_ref[hh, PAD + b0 + hb_ - 1:PAD + b0 + hb_, :] for b0 in range(g0, g0 + 8, blk)]
                    mg = jnp.broadcast_to(rows_m[-1], (8, dk))
                    for j in range(len(rows_m) - 2, -1, -1):
                        mg = jnp.where(sub8 < (j + 1) * blk, rows_m[j], mg)
                    xs_.append(jnp.where(up8, q[g0:g0 + 8], k[g0:g0 + 8]))
                    es_.append(-jnp.abs(b2[g0:g0 + 8] - mg))
                x = jnp.concatenate(xs_, axis=0)
                e = jnp.concatenate(es_, axis=0)
            y = (x * jnp.exp2(e)).astype(BF16)
            a = a + _dot_nt(y, y) * lm_ref[lvl]
        o = o + _dot(a.astype(BF16), v.astype(BF16))

        k1, b1, v1 = kp_ref[hh, PAD - 1:PAD - 1 + C, :], bp_ref[hh, PAD - 1:PAD - 1 + C, :], vp_ref[hh, PAD - 1:PAD - 1 + C, :]
        a1 = jnp.sum(q * k1 * jnp.exp2(jnp.minimum(b2 - b1, 0.0)), axis=-1, keepdims=True)
        a0 = jnp.sum(q * k, axis=-1, keepdims=True)
        odd8 = (sub8 & 1) == 1
        a1 = jnp.concatenate([jnp.where(odd8, a1[g0:g0 + 8], 0.0) for g0 in range(0, C, 8)], axis=0)
        oc_ref[hh] = o + a1 * v1 + a0 * v

    def finish(hh, r0):
        rows, cols = pl.ds(r0, C), slice(hh * dk, (hh + 1) * dk)
        k, b2, v = kp_ref[hh, PAD:PAD + C, :], bp_ref[hh, PAD:PAD + C, :], i_ref[rows, cols]
        b_last = b2[C - 1:C, :]
        khat = k * jnp.exp2(b_last - b2)
        st_ref[hh] = st_ref[hh] * jnp.exp2(b_last) + _dot(v.T.astype(BF16), khat.astype(BF16))
        og = og_ref[rows, cols]
        y = _rms(oc_ref[hh], gain) * (og * _sigmoid(og))
        o_ref[rows, cols] = y.astype(o_ref.dtype)

    def chunk(c, carry):
        r0 = pl.multiple_of(c * C, C)
        for phase in (gates, scores, finish):
            for hh in range(n_heads):
                phase(hh, r0)
        return carry

    lax.fori_loop(0, n_chunks, chunk, 0)


def _hgrn(proj, lb_logits, gain, *, layer, batch, seq, width):
    n = proj.shape[0]
    dk = gain.shape[-1]
    heads = width // dk
    hb = _tile(heads, 16)
    hg = heads // hb
    t = _tile(seq, 512)
    ns = seq // t
    n_layers = lb_logits.shape[0]

    def col(seg):
        return pl.BlockSpec((t, hb * dk), lambda b, h, s, seg=seg: (b * ns + s, seg * hg + h))

    kern = functools.partial(_hgrn_kernel, layer=layer, n_chunks=t // HGRN_CHUNK, n_heads=hb)
    buf = pltpu.VMEM((hb, HGRN_PAD + HGRN_CHUNK, dk), F32)
    cbuf = pltpu.VMEM((hb, HGRN_CHUNK, dk), F32)
    masks = pltpu.VMEM((len(HGRN_LEVEL_BLOCKS), HGRN_CHUNK, HGRN_CHUNK), F32)
    return pl.pallas_call(
        kern,
        out_shape=jax.ShapeDtypeStruct((n, width), BF16),
        grid=(batch, hg, ns),
        in_specs=[pl.BlockSpec((n_layers, hb * dk), lambda b, h, s: (0, h)),
                  pl.BlockSpec((1, dk), lambda b, h, s: (0, 0)),
                  col(0), col(1), col(2), col(3)],
        out_specs=pl.BlockSpec((t, hb * dk), lambda b, h, s: (b * ns + s, h)),
        scratch_shapes=[pltpu.VMEM((hb, dk, dk), F32), buf, buf, buf, cbuf, masks],
        compiler_params=_params("parallel", "parallel", "arbitrary"),
        name="hgrn2",
    )(lb_logits, gain.reshape(1, dk), proj, proj, proj, proj)


def _pool_kernel(u_ref, halo_ref, w_ref, sc_ref, o_ref, *, windows):
    g = pl.program_id(2)
    s = pl.program_id(1)
    t = u_ref.shape[0]
    u = u_ref[...]
    win = jnp.int32(windows[0])
    for gi in range(1, len(windows)):
        win = jnp.where(g == gi, jnp.int32(windows[gi]), win)
    r_i = lax.broadcasted_iota(jnp.int32, (t, POOL_HALO + t), 0)
    c_i = lax.broadcasted_iota(jnp.int32, (t, POOL_HALO + t), 1) - POOL_HALO
    lag = r_i - c_i
    inside = (lag >= 0) & (lag < win) & (c_i + s * t >= 0)
    band = jnp.where(inside, 1.0, 0.0).astype(BF16)
    ext = jnp.concatenate([halo_ref[...].astype(BF16), u.astype(BF16)], axis=0)
    wsum = _dot(band, ext)
    pos = (lax.broadcasted_iota(jnp.int32, (t, 1), 0) + s * t + 1).astype(F32)
    pooled = wsum / jnp.minimum(pos, win.astype(F32)) - u
    mixed = _dot(pooled.astype(BF16), w_ref[...].astype(BF16))
    o_ref[...] = (mixed * sc_ref[...]).astype(o_ref.dtype)


def _pool(proj, w_pool, pool_scale, *, layer, batch, seq, col0):
    n = proj.shape[0]
    _, groups, cg, _ = w_pool.shape
    t = _tile(seq, 512)
    ns = seq // t
    hb = t // POOL_HALO
    assert col0 % cg == 0 and groups == len(POOL_WINDOWS)
    cb = col0 // cg
    kern = functools.partial(_pool_kernel, windows=POOL_WINDOWS)
    return pl.pallas_call(
        kern,
        out_shape=jax.ShapeDtypeStruct((n, groups * cg), BF16),
        grid=(batch, ns, groups),
        in_specs=[pl.BlockSpec((t, cg), lambda b, s, g: (b * ns + s, cb + g)),
                  pl.BlockSpec((POOL_HALO, cg),
                               lambda b, s, g: (jnp.maximum((b * ns + s) * hb - 1, 0), cb + g)),
                  pl.BlockSpec((None, None, cg, cg), lambda b, s, g: (layer, g, 0, 0)),
                  pl.BlockSpec((1, cg), lambda b, s, g: (0, g))],
        out_specs=pl.BlockSpec((t, cg), lambda b, s, g: (b * ns + s, g)),
        compiler_params=_params("parallel", "parallel", "arbitrary"),
        name="pool_mixer",
    )(proj, proj, w_pool, pool_scale[layer].reshape(1, groups * cg))


def _merge_kernel(oa_ref, ob_ref, wa_ref, wb_ref, ga_ref, gb_ref, o_ref):
    a = _dot(oa_ref[...], wa_ref[...].astype(BF16))
    b = _dot(ob_ref[...], wb_ref[...].astype(BF16))
    o_ref[...] = (_sigmoid(ga_ref[...]) * a + _sigmoid(gb_ref[...]) * b).astype(o_ref.dtype)


def _merge(o_a, o_b, w_a, w_b, proj, *, layer, col_ga, col_gb):
    n, wa = o_a.shape
    wb = o_b.shape[1]
    d = w_a.shape[2]
    bm, bn = _tile(n, 1024), _tile(d, 512)
    assert col_ga % bn == 0 and col_gb % bn == 0
    ca, cb = col_ga // bn, col_gb // bn
    return pl.pallas_call(
        _merge_kernel,
        out_shape=jax.ShapeDtypeStruct((n, d), BF16),
        grid=(n // bm, d // bn),
        in_specs=[pl.BlockSpec((bm, wa), lambda i, j: (i, 0)),
                  pl.BlockSpec((bm, wb), lambda i, j: (i, 0)),
                  pl.BlockSpec((None, wa, bn), lambda i, j: (layer, 0, j)),
                  pl.BlockSpec((None, wb, bn), lambda i, j: (layer, 0, j)),
                  pl.BlockSpec((bm, bn), lambda i, j: (i, ca + j)),
                  pl.BlockSpec((bm, bn), lambda i, j: (i, cb + j))],
        out_specs=pl.BlockSpec((bm, bn), lambda i, j: (i, j)),
        compiler_params=_params("parallel", "arbitrary"),
        name="gated_merge",
    )(o_a, o_b, w_a, w_b, proj, proj)


def _outproj_kernel(m_ref, w_ref, x_ref, o_ref):
    o_ref[...] = x_ref[...] + _dot(m_ref[...], w_ref[...].astype(BF16))


def _outproj(merged, w, x, layer):
    n, d = merged.shape
    dn = w.shape[2]
    bm, bn = _tile(n, 1024), _tile(dn, 512)
    return pl.pallas_call(
        _outproj_kernel,
        out_shape=jax.ShapeDtypeStruct((n, dn), F32),
        grid=(n // bm, dn // bn),
        in_specs=[pl.BlockSpec((bm, d), lambda i, j: (i, 0)),
                  pl.BlockSpec((None, d, bn), lambda i, j: (layer, 0, j)),
                  pl.BlockSpec((bm, bn), lambda i, j: (i, j))],
        out_specs=pl.BlockSpec((bm, bn), lambda i, j: (i, j)),
        compiler_params=_params("parallel", "arbitrary"),
        name="outproj_residual",
    )(merged, w, x)


def _router_kernel(x_ref, g_ref, wr_ref, hp_ref, route_ref, cnt_ref, run_ref, *, n_groups, n_experts):
    tr, d = x_ref.shape
    half = d // 2

    @pl.when(pl.program_id(0) == 0)
    def _():
        run_ref[...] = jnp.zeros_like(run_ref)

    h = _rms(x_ref[...], g_ref[...])
    hb = h.astype(BF16)
    hp_ref[...] = _pack_halves(h[:, :half], h[:, half:])

    h_lo = (h - hb.astype(F32)).astype(BF16)
    wr = wr_ref[...]
    w_hi = wr.astype(BF16)
    w_lo = (wr - w_hi.astype(F32)).astype(BF16)
    logits = _dot(hb, w_hi) + _dot(hb, w_lo) + _dot(h_lo, w_hi)

    lane = lax.broadcasted_iota(jnp.int32, (tr, LANES), 1).astype(F32)
    ninf = jnp.float32(-jnp.inf)
    big = jnp.float32(1e9)
    is_g = lane < n_groups
    gl = jnp.where(is_g, logits, ninf)
    gmax = jnp.max(gl, axis=-1, keepdims=True)
    gidx = jnp.min(jnp.where(gl == gmax, lane, big), axis=-1, keepdims=True)
    g_w = 1.0 / jnp.sum(jnp.where(is_g, jnp.exp(gl - gmax), 0.0), axis=-1, keepdims=True)

    e_lo = n_groups + gidx * n_experts
    el = jnp.where(lane >= e_lo, jnp.where(lane < e_lo + n_experts, logits, ninf), ninf)
    v1 = jnp.max(el, axis=-1, keepdims=True)
    i1 = jnp.min(jnp.where(el == v1, lane, big), axis=-1, keepdims=True)
    el2 = jnp.where(lane == i1, ninf, el)
    v2 = jnp.max(el2, axis=-1, keepdims=True)
    i2 = jnp.min(jnp.where(el2 == v2, lane, big), axis=-1, keepdims=True)
    tt = jnp.exp(v2 - v1)
    w1 = g_w / (1.0 + tt)
    w2 = g_w * tt / (1.0 + tt)
    e1 = i1 - n_groups
    e2 = i2 - n_groups

    oh1 = lane == e1
    oh2 = lane == e2
    cnt = jnp.where(oh1, 1.0, 0.0) + jnp.where(oh2, 1.0, 0.0)
    r_i = lax.broadcasted_iota(jnp.int32, (tr, tr), 0)
    c_i = lax.broadcasted_iota(jnp.int32, (tr, tr), 1)
    strict = jnp.where(r_i > c_i, 1.0, 0.0).astype(BF16)
    before = _dot(strict, cnt.astype(BF16)) + run_ref[0:1, :]
    rank1 = jnp.sum(jnp.where(oh1, before, 0.0), axis=-1, keepdims=True)
    rank2 = jnp.sum(jnp.where(oh2, before, 0.0), axis=-1, keepdims=True)
    run = run_ref[0:1, :] + jnp.sum(cnt, axis=0, keepdims=True)
    run_ref[...] = jnp.broadcast_to(run, run_ref.shape)
    cnt_ref[...] = jnp.broadcast_to(run, cnt_ref.shape)

    out = jnp.zeros((tr, LANES), F32)
    for idx, val in enumerate((e1, e2, w1, w2, rank1, rank2)):
        out = jnp.where(lane == idx, val, out)
    route_ref[...] = out[:, :ROUTE_COLS]


def _router(x, g, w_router, *, n_groups, n_experts):
    n, d = x.shape
    tr = _tile(n, 256)
    kern = functools.partial(_router_kernel, n_groups=n_groups, n_experts=n_experts)
    return pl.pallas_call(
        kern,
        out_shape=(jax.ShapeDtypeStruct((n, d // 2), jnp.uint32),
                   jax.ShapeDtypeStruct((n, ROUTE_COLS), F32),
                   jax.ShapeDtypeStruct((8, LANES), F32)),
        grid=(n // tr,),
        in_specs=[pl.BlockSpec((tr, d), lambda i: (i, 0)),
                  pl.BlockSpec((1, d), lambda i: (0, 0)),
                  pl.BlockSpec((d, LANES), lambda i: (0, 0))],
        out_specs=(pl.BlockSpec((tr, d // 2), lambda i: (i, 0)),
                   pl.BlockSpec((tr, ROUTE_COLS), lambda i: (i, 0)),
                   pl.BlockSpec((8, LANES), lambda i: (0, 0))),
        scratch_shapes=[pltpu.VMEM((8, LANES), F32)],
        compiler_params=_params("arbitrary"),
        name="norm_router",
    )(x, g.reshape(1, d), w_router)


def _row_copy(src_ref, src_row, dst_ref, dst_row, sem):
    return pltpu.make_async_copy(src_ref.at[pl.ds(src_row, 1)], dst_ref.at[pl.ds(dst_row, 1)], sem)


def _dispatch_kernel(d1_ref, d2_ref, zb_ref, zf_ref, h_ref, xs_ref, zero_ref, sem, *, n_exp, tr):
    sub = zero_ref.shape[0]
    step = pl.program_id(0)

    def zero_copy(e):
        dst = xs_ref.at[pl.ds(pl.multiple_of(zb_ref[e] * sub, sub), sub)]
        return pltpu.make_async_copy(zero_ref, dst, sem.at[2])

    @pl.when(pl.program_id(0) == 0)
    def _():
        zero_ref[...] = jnp.zeros_like(zero_ref)

        def z_start(e, carry):
            @pl.when(zf_ref[e] != 0)
            def _():
                zero_copy(e).start()
            return carry

        def z_wait(e, carry):
            @pl.when(zf_ref[e] != 0)
            def _():
                zero_copy(e).wait()
            return carry

        lax.fori_loop(0, n_exp, z_start, 0)
        lax.fori_loop(0, n_exp, z_wait, 0)

    def copies(blk, r):
        t = blk * tr + r
        return (_row_copy(h_ref, t, xs_ref, d1_ref[t], sem.at[0]),
                _row_copy(h_ref, t, xs_ref, d2_ref[t], sem.at[1]))

    def issue(r, carry):
        c1, c2 = copies(step, r)
        c1.start()
        c2.start()
        return carry

    def drain(blk):
        def body(r, carry):
            c1, c2 = copies(blk, r)
            c1.wait()
            c2.wait()
            return carry
        lax.fori_loop(0, tr, body, 0)

    lax.fori_loop(0, tr, issue, 0)

    @pl.when(step > 0)
    def _():
        drain(step - 1)

    @pl.when(step == pl.num_programs(0) - 1)
    def _():
        drain(step)


def _dispatch(hp, dest1, dest2, zero_block, zero_flag, *, rows, sub):
    n, dw = hp.shape
    tr = _tile(n, 256)
    grid_spec = pltpu.PrefetchScalarGridSpec(
        num_scalar_prefetch=4,
        grid=(n // tr,),
        in_specs=[pl.BlockSpec(memory_space=pl.ANY)],
        out_specs=pl.BlockSpec(memory_space=pl.ANY),
        scratch_shapes=[pltpu.VMEM((sub, dw), hp.dtype), pltpu.SemaphoreType.DMA((3,))],
    )
    return pl.pallas_call(
        functools.partial(_dispatch_kernel, n_exp=zero_block.shape[0], tr=tr),
        out_shape=jax.ShapeDtypeStruct((rows, dw), hp.dtype),
        grid_spec=grid_spec,
        compiler_params=_params("arbitrary"),
        name="dispatch",
    )(dest1, dest2, zero_block, zero_flag, hp)


def _pack_halves(lo, hi):
    lo_w = pltpu.bitcast(lo.astype(BF16).astype(F32), jnp.uint32) >> 16
    hi_w = pltpu.bitcast(hi.astype(BF16).astype(F32), jnp.uint32) & jnp.uint32(0xFFFF0000)
    return lo_w | hi_w


def _unpack_halves(w):
    return (pltpu.bitcast(w << 16, F32), pltpu.bitcast(w & jnp.uint32(0xFFFF0000), F32))


def _expert_kernel(se_ref, sb_ref, sn_ref, ns_ref, xs_ref, wg_ref, wu_ref, wdl_ref, wdh_ref, o_ref,
                   hid_ref, *, n_up, sub):
    del se_ref, sb_ref
    s, p = pl.program_id(0), pl.program_id(1)
    valid = s < ns_ref[0]
    n_sub = sn_ref[s]
    half = xs_ref.shape[1]
    fc = wg_ref.shape[1]

    def sub_rows(q):
        return pl.ds(pl.multiple_of(q * sub, sub), sub)

    @pl.when(jnp.logical_and(valid, p < n_up))
    def _():
        wg = wg_ref[...].astype(BF16)
        wu = wu_ref[...].astype(BF16)

        def body(q, carry):
            rows = sub_rows(q)
            x_lo, x_hi = _unpack_halves(xs_ref[rows, :])
            x_lo, x_hi = x_lo.astype(BF16), x_hi.astype(BF16)
            a = _dot(x_lo, wg[:half]) + _dot(x_hi, wg[half:])
            b = _dot(x_lo, wu[:half]) + _dot(x_hi, wu[half:])
            hid_ref[p, rows, :] = (a * _sigmoid(a) * b).astype(BF16)
            return carry

        lax.fori_loop(0, n_sub, body, 0)

    @pl.when(jnp.logical_and(valid, p >= n_up))
    def _():
        wl = wdl_ref[...].astype(BF16)
        wh = wdh_ref[...].astype(BF16)

        def body(q, carry):
            rows = sub_rows(q)
            y_lo = _dot(hid_ref[0, rows, :], wl[0:fc])
            y_hi = _dot(hid_ref[0, rows, :], wh[0:fc])
            for j in range(1, n_up):
                y_lo += _dot(hid_ref[j, rows, :], wl[j * fc:(j + 1) * fc])
                y_hi += _dot(hid_ref[j, rows, :], wh[j * fc:(j + 1) * fc])
            o_ref[rows, :] = _pack_halves(y_lo, y_hi)
            return carry

        def zero_body(q, carry):
            o_ref[sub_rows(q), :] = jnp.zeros((sub, o_ref.shape[1]), o_ref.dtype)
            return carry

        lax.fori_loop(0, n_sub, body, 0)
        lax.fori_loop(n_sub, xs_ref.shape[0] // sub, zero_body, 0)


def _experts(xs, w_gate, w_up, w_down, sup_expert, sup_block, sup_nsub, n_super, *, sub, sup):
    rows, dw = xs.shape
    _, d, f = w_gate.shape
    fc = _tile(f, 256)
    n_up = f // fc
    dc = _tile(dw, min(512, dw // 2))
    n_down = dw // dc
    n_phase = n_up + n_down

    def phase(s, p, ns):
        return jnp.where(s < ns[0], p, n_phase - 1)

    def x_map(s, p, se, sb, sn, ns):
        return (sb[s + (phase(s, p, ns) >= n_up).astype(jnp.int32)], 0)

    def up_map(first_down_phase):
        def index_map(s, p, se, sb, sn, ns):
            ph = phase(s, p, ns)
            ahead = ph >= n_up + first_down_phase
            return (se[s + ahead.astype(jnp.int32)], 0, jnp.where(ahead, 0, jnp.minimum(ph, n_up - 1)))
        return index_map

    def down_c(s, p, ns):
        return jnp.maximum(phase(s, p, ns) - n_up, 0)

    def down_map(col0):
        def index_map(s, p, se, sb, sn, ns):
            early = jnp.logical_and(phase(s, p, ns) < n_up - 1, s > 0)
            expert = se[jnp.where(early, s - 1, s)]
            return (expert, 0, col0 + jnp.where(early, n_down - 1, down_c(s, p, ns)))
        return index_map

    grid_spec = pltpu.PrefetchScalarGridSpec(
        num_scalar_prefetch=4,
        grid=(rows // sup, n_phase),
        in_specs=[pl.BlockSpec((sup, dw), x_map),
                  pl.BlockSpec((None, d, fc), up_map(0)),
                  pl.BlockSpec((None, d, fc), up_map(1)),
                  pl.BlockSpec((None, f, dc), down_map(0)),
                  pl.BlockSpec((None, f, dc), down_map(n_down))],
        out_specs=pl.BlockSpec((sup, dc), lambda s, p, se, sb, sn, ns: (sb[s], down_c(s, p, ns))),
        scratch_shapes=[pltpu.VMEM((n_up, sup, fc), BF16)],
    )
    return pl.pallas_call(
        functools.partial(_expert_kernel, n_up=n_up, sub=sub),
        out_shape=jax.ShapeDtypeStruct((rows, dw), jnp.uint32),
        grid_spec=grid_spec,
        compiler_params=_params("arbitrary", "arbitrary"),
        name="expert_mlp",
    )(sup_expert, sup_block, sup_nsub, n_super, xs, w_gate, w_up, w_down, w_down)


def _combine_kernel(d1_ref, d2_ref, x_ref, route_ref, g_ref, ys_ref, *rest, final):
    if final:
        y_ref, buf1, buf2, sem = rest
    else:
        x2_ref, hn_ref, buf1, buf2, sem = rest
    tr = x_ref.shape[0]
    step = pl.program_id(0)
    slot = step % 2

    def copies(blk, r):
        t = blk * tr + r
        s = blk % 2
        return (_row_copy(ys_ref, d1_ref[t], buf1.at[s], r, sem.at[s, 0]),
                _row_copy(ys_ref, d2_ref[t], buf2.at[s], r, sem.at[s, 1]))

    def gather(blk, start):
        def body(r, carry):
            c1, c2 = copies(blk, r)
            if start:
                c1.start()
                c2.start()
            else:
                c1.wait()
                c2.wait()
            return carry
        lax.fori_loop(0, tr, body, 0)

    @pl.when(step == 0)
    def _():
        gather(step, True)

    @pl.when(step + 1 < pl.num_programs(0))
    def _():
        gather(step + 1, True)

    gather(step, False)

    route = route_ref[...]
    w1, w2 = route[:, 2:3], route[:, 3:4]
    half = buf1.shape[2]
    d = 2 * half
    lo1, hi1 = _unpack_halves(buf1[slot])
    lo2, hi2 = _unpack_halves(buf2[slot])
    x2_lo = x_ref[:, :half] + w1 * lo1 + w2 * lo2
    x2_hi = x_ref[:, half:] + w1 * hi1 + w2 * hi2
    ms = (jnp.sum(x2_lo * x2_lo, axis=-1, keepdims=True) +
          jnp.sum(x2_hi * x2_hi, axis=-1, keepdims=True)) * (1.0 / d)
    inv = lax.rsqrt(ms + RMS_EPS)
    out_ref = y_ref if final else hn_ref
    if not final:
        x2_ref[:, :half] = x2_lo
        x2_ref[:, half:] = x2_hi
    out_ref[:, :half] = (x2_lo * inv * g_ref[:, :half]).astype(out_ref.dtype)
    out_ref[:, half:] = (x2_hi * inv * g_ref[:, half:]).astype(out_ref.dtype)


def _combine(x, route, g, ys, dest1, dest2, *, final):
    n, d = x.shape
    tr = _tile(n, 256)
    row = lambda i, d1, d2: (i, 0)
    if final:
        out_shape = jax.ShapeDtypeStruct((n, d), F32)
        out_specs = pl.BlockSpec((tr, d), row)
    else:
        out_shape = (jax.ShapeDtypeStruct((n, d), F32), jax.ShapeDtypeStruct((n, d), BF16))
        out_specs = (pl.BlockSpec((tr, d), row), pl.BlockSpec((tr, d), row))
    grid_spec = pltpu.PrefetchScalarGridSpec(
        num_scalar_prefetch=2,
        grid=(n // tr,),
        in_specs=[pl.BlockSpec((tr, d), row),
                  pl.BlockSpec((tr, ROUTE_COLS), row),
                  pl.BlockSpec((1, d), lambda i, d1, d2: (0, 0)),
                  pl.BlockSpec(memory_space=pl.ANY)],
        out_specs=out_specs,
        scratch_shapes=[pltpu.VMEM((2, tr, d // 2), jnp.uint32), pltpu.VMEM((2, tr, d // 2), jnp.uint32),
                        pltpu.SemaphoreType.DMA((2, 2))],
    )
    return pl.pallas_call(
        functools.partial(_combine_kernel, final=final),
        out_shape=out_shape,
        grid_spec=grid_spec,
        compiler_params=_params("arbitrary"),
        name="combine_final" if final else "combine",
    )(dest1, dest2, x, route, g.reshape(1, d), ys)


def _plan(route, counts, *, ge, sub, sup, max_super):
    i32 = jnp.int32
    e1, e2 = route[:, 0].astype(i32), route[:, 1].astype(i32)
    rank1, rank2 = route[:, 4].astype(i32), route[:, 5].astype(i32)
    cnt = counts[0, :ge].astype(i32)
    n_sub = (cnt + sub - 1) // sub
    n_sup = (cnt + sup - 1) // sup
    ends = jnp.cumsum(n_sup * sup)
    offs = ends - n_sup * sup
    dest1 = offs[e1] + rank1
    dest2 = offs[e2] + rank2
    sup_ends = jnp.cumsum(n_sup)
    n_super = sup_ends[-1]
    s = jnp.arange(max_super + 1, dtype=i32)
    se = jnp.minimum(jnp.sum((sup_ends[None, :] <= s[:, None]).astype(i32), axis=1), ge - 1)
    k = s - (sup_ends[se] - n_sup[se])
    sb = offs[se] // sup + k
    sn = jnp.clip(n_sub[se] - k * (sup // sub), 0, sup // sub)
    last = jnp.maximum(n_super - 1, 0)
    valid = s < n_super
    sup_expert = jnp.where(valid, se, se[last]).astype(i32)
    sup_block = jnp.where(valid, sb, sb[last]).astype(i32)
    sup_nsub = jnp.where(valid, sn, 0).astype(i32)
    zero_block = (offs // sub + cnt // sub).astype(i32)
    zero_flag = (cnt % sub != 0).astype(i32)
    return (dest1, dest2, sup_expert, sup_block, sup_nsub, n_super.reshape(1).astype(i32),
            zero_block, zero_flag)


def kernel(x, norm1_g, w_in, lb_logits, hgrn_norm_g, w_pool, pool_scale, w_branch_a, w_branch_b,
           w_out, norm2_g, w_router_group, w_router_expert, w_gate, w_up, w_down, final_norm_g):
    batch, seq, d = x.shape
    n = batch * seq
    depth = w_in.shape[0]
    width = lb_logits.shape[1]
    pool_width = pool_scale.shape[1]
    n_groups = w_router_group.shape[-1]
    n_experts = w_gate.shape[2]
    ge = n_groups * n_experts
    d_expert = w_gate.shape[-1]
    assert n_groups + ge <= LANES
    col_pool = 4 * width
    col_ga = col_pool + pool_width
    col_gb = col_ga + d

    sub = MOE_SUB_ROWS if n >= 4096 else 32
    sup = MOE_SUBS_PER_SUPER * sub
    max_super = (TOP_K * n + ge * (sup - 1)) // sup
    rows = max_super * sup

    wg_all = w_gate.reshape(depth * ge, d, d_expert)
    wu_all = w_up.reshape(depth * ge, d, d_expert)
    wd_all = w_down.reshape(depth * ge, d_expert, d)

    xf = x.reshape(n, d)
    h = _rmsnorm(xf, norm1_g[0], BF16)
    for l in range(depth):
        proj = _inproj(h, w_in, l)
        o_a = _hgrn(proj, lb_logits, hgrn_norm_g[l], layer=l, batch=batch, seq=seq, width=width)
        o_b = _pool(proj, w_pool, pool_scale, layer=l, batch=batch, seq=seq, col0=col_pool)
        merged = _merge(o_a, o_b, w_branch_a, w_branch_b, proj, layer=l, col_ga=col_ga, col_gb=col_gb)
        x1 = _outproj(merged, w_out, xf, l)

        w_router = jnp.concatenate([w_router_group[l], w_router_expert[l]], axis=1)
        w_router = jnp.pad(w_router, ((0, 0), (0, LANES - w_router.shape[1])))
        hp, route, counts = _router(x1, norm2_g[l], w_router, n_groups=n_groups, n_experts=n_experts)
        (dest1, dest2, sup_expert, sup_block, sup_nsub, n_super, zero_block, zero_flag) = _plan(
            route, counts, ge=ge, sub=sub, sup=sup, max_super=max_super)
        xs = _dispatch(hp, dest1, dest2, zero_block, zero_flag, rows=rows, sub=sub)
        ys = _experts(xs, wg_all, wu_all, wd_all, sup_expert + l * ge, sup_block, sup_nsub, n_super,
                      sub=sub, sup=sup)
        if l + 1 < depth:
            xf, h = _combine(x1, route, norm1_g[l + 1], ys, dest1, dest2, final=False)
        else:
            out = _combine(x1, route, final_norm_g, ys, dest1, dest2, final=True)
    return out.reshape(batch, seq, d)
```

```python
import functools

import jax
import jax.numpy as jnp
from jax import lax
from jax.experimental import pallas as pl
from jax.experimental.pallas import tpu as pltpu

RMS_EPS = 1e-6
MIN_FORGET = 1e-30
LOG2E = 1.4426950408889634
POOL_WINDOWS = (2, 4, 8, 16)
POOL_HALO = 16
HGRN_CHUNK = 64
HGRN_PAD = 8
HGRN_LEVEL_BLOCKS = (64, 32, 16, 8, 4)
TOP_K = 2
MOE_SUB_ROWS = 256
MOE_SUBS_PER_SUPER = 4
LANES = 128
ROUTE_COLS = 8
ROUTE_CODE_SHIFT = 20
VMEM_LIMIT_BYTES = 56 * 1024 * 1024

F32 = jnp.float32
BF16 = jnp.bfloat16


def _params(*sem):
    return pltpu.CompilerParams(dimension_semantics=sem, vmem_limit_bytes=VMEM_LIMIT_BYTES)


def _tile(n, pref):
    t = min(n, pref)
    assert n % t == 0, (n, pref)
    return t


def _sigmoid(z):
    return 1.0 / (1.0 + jnp.exp(-z))


def _dot(a, b):
    return jnp.dot(a, b, preferred_element_type=F32)


def _dot_nt(a, b):
    return lax.dot_general(a, b, (((1,), (1,)), ((), ())), preferred_element_type=F32)


def _rms(x, g):
    return x * lax.rsqrt(jnp.mean(x * x, axis=-1, keepdims=True) + RMS_EPS) * g


def _rmsnorm_kernel(x_ref, g_ref, o_ref):
    o_ref[...] = _rms(x_ref[...], g_ref[...]).astype(o_ref.dtype)


def _rmsnorm(x, g, out_dtype):
    n, d = x.shape
    tr = _tile(n, 256)
    return pl.pallas_call(
        _rmsnorm_kernel,
        out_shape=jax.ShapeDtypeStruct((n, d), out_dtype),
        grid=(n // tr,),
        in_specs=[pl.BlockSpec((tr, d), lambda i: (i, 0)),
                  pl.BlockSpec((1, d), lambda i: (0, 0))],
        out_specs=pl.BlockSpec((tr, d), lambda i: (i, 0)),
        compiler_params=_params("parallel"),
        name="rmsnorm",
    )(x, g.reshape(1, d))


def _inproj_kernel(h_ref, w_ref, o_ref):
    o_ref[...] = _dot(h_ref[...], w_ref[...].astype(BF16))


def _inproj(h, w, layer):
    n, d = h.shape
    cols = w.shape[2]
    bm, bn = _tile(n, 1024), _tile(cols, 512)
    return pl.pallas_call(
        _inproj_kernel,
        out_shape=jax.ShapeDtypeStruct((n, cols), F32),
        grid=(n // bm, cols // bn),
        in_specs=[pl.BlockSpec((bm, d), lambda i, j: (i, 0)),
                  pl.BlockSpec((None, d, bn), lambda i, j: (layer, 0, j))],
        out_specs=pl.BlockSpec((bm, bn), lambda i, j: (i, j)),
        compiler_params=_params("parallel", "arbitrary"),
        name="inproj",
    )(h, w)


def _split3(x):
    hi = x.astype(BF16)
    r1 = x - hi.astype(F32)
    mid = r1.astype(BF16)
    lo = (r1 - mid.astype(F32)).astype(BF16)
    return hi, mid, lo


def _hgrn_kernel(lbl_ref, g_ref, q_ref, f_ref, i_ref, og_ref, o_ref,
                 st_ref, kp_ref, bp_ref, vp_ref, oc_ref, lm_ref, *, layer, n_chunks, n_heads):
    C, PAD = HGRN_CHUNK, HGRN_PAD
    LEVEL_BLOCKS = HGRN_LEVEL_BLOCKS
    dk = g_ref.shape[1]

    @pl.when(pl.program_id(2) == 0)
    def _():
        st_ref[...] = jnp.zeros_like(st_ref)

    zpad = jnp.zeros((PAD, dk), F32)
    for hh in range(n_heads):
        kp_ref[hh, 0:PAD, :] = zpad
        bp_ref[hh, 0:PAD, :] = zpad
        vp_ref[hh, 0:PAD, :] = zpad

    t_i = lax.broadcasted_iota(jnp.int32, (C, C), 0)
    s_i = lax.broadcasted_iota(jnp.int32, (C, C), 1)
    for lvl, blk in enumerate(LEVEL_BLOCKS):
        same = (t_i // blk) == (s_i // blk)
        take = same & ((t_i % blk) >= blk // 2) & ((s_i % blk) < blk // 2)
        lm_ref[lvl] = jnp.where(take, 1.0, 0.0)
    sub8 = lax.broadcasted_iota(jnp.int32, (8, dk), 0)

    lbl = lbl_ref[...]
    e = jnp.exp(lbl - jnp.max(lbl, axis=0, keepdims=True))
    p = e / jnp.sum(e, axis=0, keepdims=True)
    lb_all = jnp.sum(p[0:layer + 1], axis=0, keepdims=True) - p[0:1]
    gain = g_ref[...]

    tri = jnp.where(t_i >= s_i, 1.0, 0.0).astype(BF16)
    tri3 = jnp.concatenate([tri, tri, tri], axis=1)


    def gates(hh, r0):
        rows, cols = pl.ds(r0, C), slice(hh * dk, (hh + 1) * dk)
        lb = lb_all[:, cols]
        z = f_ref[rows, cols]
        f = lb + (1.0 - lb) * _sigmoid(z)
        lf = jnp.log(jnp.maximum(f, MIN_FORGET))
        k = (1.0 - lb) * _sigmoid(-z)
        hi, mid, lo = _split3(lf)
        b2 = _dot(tri3, jnp.concatenate([hi, mid, lo], axis=0)) * LOG2E
        kp_ref[hh, PAD:PAD + C, :] = k
        bp_ref[hh, PAD:PAD + C, :] = b2
        vp_ref[hh, PAD:PAD + C, :] = i_ref[rows, cols]

    def scores(hh, r0):
        rows, cols = pl.ds(r0, C), slice(hh * dk, (hh + 1) * dk)
        q, v = q_ref[rows, cols], i_ref[rows, cols]
        k, b2 = kp_ref[hh, PAD:PAD + C, :], bp_ref[hh, PAD:PAD + C, :]
        o = _dot_nt((q * jnp.exp2(b2)).astype(BF16), st_ref[hh].astype(BF16))

        a = jnp.zeros((C, C), F32)
        for lvl, blk in enumerate(LEVEL_BLOCKS):
            hb_ = blk // 2
            if hb_ >= 8:
                xs_, es_ = [], []
                for b0 in range(0, C, blk):
                    m = bp_ref[hh, PAD + b0 + hb_ - 1:PAD + b0 + hb_, :]
                    xs_ += [k[b0:b0 + hb_], q[b0 + hb_:b0 + blk]]
                    es_ += [m - b2[b0:b0 + hb_], b2[b0 + hb_:b0 + blk] - m]
                x = jnp.concatenate(xs_, axis=0)
                e = jnp.minimum(jnp.concatenate(es_, axis=0), 0.0)
            else:
                up8 = (sub8 & (blk - 1)) >= hb_
                xs_, es_ = [], []
                for g0 in range(0, C, 8):
                    rows_m = [bp_ref[hh, PAD + b0 + hb_ - 1:PAD + b0 + hb_, :] for b0 in range(g0, g0 + 8, blk)]
                    mg = jnp.broadcast_to(rows_m[-1], (8, dk))
                    for j in range(len(rows_m) - 2, -1, -1):
                        mg = jnp.where(sub8 < (j + 1) * blk, rows_m[j], mg)
                    xs_.append(jnp.where(up8, q[g0:g0 + 8], k[g0:g0 + 8]))
                    es_.append(-jnp.abs(b2[g0:g0 + 8] - mg))
                x = jnp.concatenate(xs_, axis=0)
                e = jnp.concatenate(es_, axis=0)
            y = (x * jnp.exp2(e)).astype(BF16)
            a = a + _dot_nt(y, y) * lm_ref[lvl]
        o = o + _dot(a.astype(BF16), v.astype(BF16))

        k1, b1, v1 = kp_ref[hh, PAD - 1:PAD - 1 + C, :], bp_ref[hh, PAD - 1:PAD - 1 + C, :], vp_ref[hh, PAD - 1:PAD - 1 + C, :]
        a1 = jnp.sum(q * k1 * jnp.exp2(jnp.minimum(b2 - b1, 0.0)), axis=-1, keepdims=True)
        a0 = jnp.sum(q * k, axis=-1, keepdims=True)
        odd8 = (sub8 & 1) == 1
        a1 = jnp.concatenate([jnp.where(odd8, a1[g0:g0 + 8], 0.0) for g0 in range(0, C, 8)], axis=0)
        oc_ref[hh] = o + a1 * v1 + a0 * v

    def finish(hh, r0):
        rows, cols = pl.ds(r0, C), slice(hh * dk, (hh + 1) * dk)
        k, b2, v = kp_ref[hh, PAD:PAD + C, :], bp_ref[hh, PAD:PAD + C, :], i_ref[rows, cols]
        b_last = b2[C - 1:C, :]
        khat = k * jnp.exp2(b_last - b2)
        st_ref[hh] = st_ref[hh] * jnp.exp2(b_last) + _dot(v.T.astype(BF16), khat.astype(BF16))
        og = og_ref[rows, cols]
        y = _rms(oc_ref[hh], gain) * (og * _sigmoid(og))
        o_ref[rows, cols] = y.astype(o_ref.dtype)

    def chunk(c, carry):
        r0 = pl.multiple_of(c * C, C)
        for phase in (gates, scores, finish):
            for hh in range(n_heads):
                phase(hh, r0)
        return carry

    lax.fori_loop(0, n_chunks, chunk, 0)


def _hgrn(proj, lb_logits, gain, *, layer, batch, seq, width):
    n = proj.shape[0]
    dk = gain.shape[-1]
    heads = width // dk
    hb = _tile(heads, 16)
    hg = heads // hb
    t = _tile(seq, 512)
    ns = seq // t
    n_layers = lb_logits.shape[0]

    def col(seg):
        return pl.BlockSpec((t, hb * dk), lambda b, h, s, seg=seg: (b * ns + s, seg * hg + h))

    kern = functools.partial(_hgrn_kernel, layer=layer, n_chunks=t // HGRN_CHUNK, n_heads=hb)
    buf = pltpu.VMEM((hb, HGRN_PAD + HGRN_CHUNK, dk), F32)
    cbuf = pltpu.VMEM((hb, HGRN_CHUNK, dk), F32)
    masks = pltpu.VMEM((len(HGRN_LEVEL_BLOCKS), HGRN_CHUNK, HGRN_CHUNK), F32)
    return pl.pallas_call(
        kern,
        out_shape=jax.ShapeDtypeStruct((n, width), BF16),
        grid=(batch, hg, ns),
        in_specs=[pl.BlockSpec((n_layers, hb * dk), lambda b, h, s: (0, h)),
                  pl.BlockSpec((1, dk), lambda b, h, s: (0, 0)),
                  col(0), col(1), col(2), col(3)],
        out_specs=pl.BlockSpec((t, hb * dk), lambda b, h, s: (b * ns + s, h)),
        scratch_shapes=[pltpu.VMEM((hb, dk, dk), F32), buf, buf, buf, cbuf, masks],
        compiler_params=_params("parallel", "parallel", "arbitrary"),
        name="hgrn2",
    )(lb_logits, gain.reshape(1, dk), proj, proj, proj, proj)


def _pool_kernel(u_ref, halo_ref, w_ref, sc_ref, o_ref, *, windows):
    g = pl.program_id(2)
    s = pl.program_id(1)
    t = u_ref.shape[0]
    u = u_ref[...]
    win = jnp.int32(windows[0])
    for gi in range(1, len(windows)):
        win = jnp.where(g == gi, jnp.int32(windows[gi]), win)
    r_i = lax.broadcasted_iota(jnp.int32, (t, POOL_HALO + t), 0)
    c_i = lax.broadcasted_iota(jnp.int32, (t, POOL_HALO + t), 1) - POOL_HALO
    lag = r_i - c_i
    inside = (lag >= 0) & (lag < win) & (c_i + s * t >= 0)
    band = jnp.where(inside, 1.0, 0.0).astype(BF16)
    ext = jnp.concatenate([halo_ref[...].astype(BF16), u.astype(BF16)], axis=0)
    wsum = _dot(band, ext)
    pos = (lax.broadcasted_iota(jnp.int32, (t, 1), 0) + s * t + 1).astype(F32)
    pooled = wsum / jnp.minimum(pos, win.astype(F32)) - u
    mixed = _dot(pooled.astype(BF16), w_ref[...].astype(BF16))
    o_ref[...] = (mixed * sc_ref[...]).astype(o_ref.dtype)


def _pool(proj, w_pool, pool_scale, *, layer, batch, seq, col0):
    n = proj.shape[0]
    _, groups, cg, _ = w_pool.shape
    t = _tile(seq, 512)
    ns = seq // t
    hb = t // POOL_HALO
    assert col0 % cg == 0 and groups == len(POOL_WINDOWS)
    cb = col0 // cg
    kern = functools.partial(_pool_kernel, windows=POOL_WINDOWS)
    return pl.pallas_call(
        kern,
        out_shape=jax.ShapeDtypeStruct((n, groups * cg), BF16),
        grid=(batch, ns, groups),
        in_specs=[pl.BlockSpec((t, cg), lambda b, s, g: (b * ns + s, cb + g)),
                  pl.BlockSpec((POOL_HALO, cg),
                               lambda b, s, g: (jnp.maximum((b * ns + s) * hb - 1, 0), cb + g)),
                  pl.BlockSpec((None, None, cg, cg), lambda b, s, g: (layer, g, 0, 0)),
                  pl.BlockSpec((1, cg), lambda b, s, g: (0, g))],
        out_specs=pl.BlockSpec((t, cg), lambda b, s, g: (b * ns + s, g)),
        compiler_params=_params("parallel", "parallel", "arbitrary"),
        name="pool_mixer",
    )(proj, proj, w_pool, pool_scale[layer].reshape(1, groups * cg))


def _merge_kernel(oa_ref, ob_ref, wa_ref, wb_ref, ga_ref, gb_ref, o_ref):
    a = _dot(oa_ref[...], wa_ref[...].astype(BF16))
    b = _dot(ob_ref[...], wb_ref[...].astype(BF16))
    o_ref[...] = (_sigmoid(ga_ref[...]) * a + _sigmoid(gb_ref[...]) * b).astype(o_ref.dtype)


def _merge(o_a, o_b, w_a, w_b, proj, *, layer, col_ga, col_gb):
    n, wa = o_a.shape
    wb = o_b.shape[1]
    d = w_a.shape[2]
    bm, bn = _tile(n, 1024), _tile(d, 512)
    assert col_ga % bn == 0 and col_gb % bn == 0
    ca, cb = col_ga // bn, col_gb // bn
    return pl.pallas_call(
        _merge_kernel,
        out_shape=jax.ShapeDtypeStruct((n, d), BF16),
        grid=(n // bm, d // bn),
        in_specs=[pl.BlockSpec((bm, wa), lambda i, j: (i, 0)),
                  pl.BlockSpec((bm, wb), lambda i, j: (i, 0)),
                  pl.BlockSpec((None, wa, bn), lambda i, j: (layer, 0, j)),
                  pl.BlockSpec((None, wb, bn), lambda i, j: (layer, 0, j)),
                  pl.BlockSpec((bm, bn), lambda i, j: (i, ca + j)),
                  pl.BlockSpec((bm, bn), lambda i, j: (i, cb + j))],
        out_specs=pl.BlockSpec((bm, bn), lambda i, j: (i, j)),
        compiler_params=_params("parallel", "arbitrary"),
        name="gated_merge",
    )(o_a, o_b, w_a, w_b, proj, proj)


def _outproj_kernel(m_ref, w_ref, x_ref, o_ref):
    o_ref[...] = x_ref[...] + _dot(m_ref[...], w_ref[...].astype(BF16))


def _outproj(merged, w, x, layer):
    n, d = merged.shape
    dn = w.shape[2]
    bm, bn = _tile(n, 1024), _tile(dn, 512)
    return pl.pallas_call(
        _outproj_kernel,
        out_shape=jax.ShapeDtypeStruct((n, dn), F32),
        grid=(n // bm, dn // bn),
        in_specs=[pl.BlockSpec((bm, d), lambda i, j: (i, 0)),
                  pl.BlockSpec((None, d, bn), lambda i, j: (layer, 0, j)),
                  pl.BlockSpec((bm, bn), lambda i, j: (i, j))],
        out_specs=pl.BlockSpec((bm, bn), lambda i, j: (i, j)),
        compiler_params=_params("parallel", "arbitrary"),
        name="outproj_residual",
    )(merged, w, x)


def _router_kernel(x_ref, g_ref, wr_ref, hp_ref, route_ref, cnt_ref, run_ref, *, n_groups, n_experts):
    tr, d = x_ref.shape
    half = d // 2

    @pl.when(pl.program_id(0) == 0)
    def _():
        run_ref[...] = jnp.zeros_like(run_ref)

    h = _rms(x_ref[...], g_ref[...])
    hb = h.astype(BF16)
    hp_ref[...] = _pack_halves(h[:, :half], h[:, half:])

    h_lo = (h - hb.astype(F32)).astype(BF16)
    wr = wr_ref[...]
    w_hi = wr.astype(BF16)
    w_lo = (wr - w_hi.astype(F32)).astype(BF16)
    logits = _dot(hb, w_hi) + _dot(hb, w_lo) + _dot(h_lo, w_hi)

    lane = lax.broadcasted_iota(jnp.int32, (tr, LANES), 1).astype(F32)
    ninf = jnp.float32(-jnp.inf)
    big = jnp.float32(1e9)
    is_g = lane < n_groups
    gl = jnp.where(is_g, logits, ninf)
    gmax = jnp.max(gl, axis=-1, keepdims=True)
    gidx = jnp.min(jnp.where(gl == gmax, lane, big), axis=-1, keepdims=True)
    g_w = 1.0 / jnp.sum(jnp.where(is_g, jnp.exp(gl - gmax), 0.0), axis=-1, keepdims=True)

    e_lo = n_groups + gidx * n_experts
    el = jnp.where(lane >= e_lo, jnp.where(lane < e_lo + n_experts, logits, ninf), ninf)
    v1 = jnp.max(el, axis=-1, keepdims=True)
    i1 = jnp.min(jnp.where(el == v1, lane, big), axis=-1, keepdims=True)
    el2 = jnp.where(lane == i1, ninf, el)
    v2 = jnp.max(el2, axis=-1, keepdims=True)
    i2 = jnp.min(jnp.where(el2 == v2, lane, big), axis=-1, keepdims=True)
    tt = jnp.exp(v2 - v1)
    w1 = g_w / (1.0 + tt)
    w2 = g_w * tt / (1.0 + tt)
    e1 = i1 - n_groups
    e2 = i2 - n_groups

    oh1 = lane == e1
    oh2 = lane == e2
    cnt = jnp.where(oh1, 1.0, 0.0) + jnp.where(oh2, 1.0, 0.0)
    r_i = lax.broadcasted_iota(jnp.int32, (tr, tr), 0)
    c_i = lax.broadcasted_iota(jnp.int32, (tr, tr), 1)
    strict = jnp.where(r_i > c_i, 1.0, 0.0).astype(BF16)
    before = _dot(strict, cnt.astype(BF16)) + run_ref[0:1, :]
    rank1 = jnp.sum(jnp.where(oh1, before, 0.0), axis=-1, keepdims=True)
    rank2 = jnp.sum(jnp.where(oh2, before, 0.0), axis=-1, keepdims=True)
    run = run_ref[0:1, :] + jnp.sum(cnt, axis=0, keepdims=True)
    run_ref[...] = jnp.broadcast_to(run, run_ref.shape)
    cnt_ref[...] = jnp.broadcast_to(run, cnt_ref.shape)

    out = jnp.zeros((tr, LANES), F32)
    for idx, val in enumerate((e1, e2, w1, w2, rank1, rank2)):
        out = jnp.where(lane == idx, val, out)
    route_ref[...] = out[:, :ROUTE_COLS]


def _router(x, g, w_router, *, n_groups, n_experts):
    n, d = x.shape
    tr = _tile(n, 256)
    kern = functools.partial(_router_kernel, n_groups=n_groups, n_experts=n_experts)
    return pl.pallas_call(
        kern,
        out_shape=(jax.ShapeDtypeStruct((n, d // 2), jnp.uint32),
                   jax.ShapeDtypeStruct((n, ROUTE_COLS), F32),
                   jax.ShapeDtypeStruct((8, LANES), F32)),
        grid=(n // tr,),
        in_specs=[pl.BlockSpec((tr, d), lambda i: (i, 0)),
                  pl.BlockSpec((1, d), lambda i: (0, 0)),
                  pl.BlockSpec((d, LANES), lambda i: (0, 0))],
        out_specs=(pl.BlockSpec((tr, d // 2), lambda i: (i, 0)),
                   pl.BlockSpec((tr, ROUTE_COLS), lambda i: (i, 0)),
                   pl.BlockSpec((8, LANES), lambda i: (0, 0))),
        scratch_shapes=[pltpu.VMEM((8, LANES), F32)],
        compiler_params=_params("arbitrary"),
        name="norm_router",
    )(x, g.reshape(1, d), w_router)


def _row_copy(src_ref, src_row, dst_ref, dst_row, sem):
    return pltpu.make_async_copy(src_ref.at[pl.ds(src_row, 1)], dst_ref.at[pl.ds(dst_row, 1)], sem)


def _dest_row(code_ref, offs_ref, t):
    code = code_ref[t]
    return offs_ref[code >> ROUTE_CODE_SHIFT] + (code & ((1 << ROUTE_CODE_SHIFT) - 1))


def _dispatch_kernel(c1_ref, c2_ref, offs_ref, zb_ref, zf_ref, h_ref, xs_ref, zero_ref, sem, *, n_exp):
    tr = h_ref.shape[0]
    sub = zero_ref.shape[0]
    base = pl.program_id(0) * tr

    def zero_copy(e):
        dst = xs_ref.at[pl.ds(pl.multiple_of(zb_ref[e] * sub, sub), sub)]
        return pltpu.make_async_copy(zero_ref, dst, sem.at[2])

    @pl.when(pl.program_id(0) == 0)
    def _():
        zero_ref[...] = jnp.zeros_like(zero_ref)

        def z_start(e, carry):
            @pl.when(zf_ref[e] != 0)
            def _():
                zero_copy(e).start()
            return carry

        def z_wait(e, carry):
            @pl.when(zf_ref[e] != 0)
            def _():
                zero_copy(e).wait()
            return carry

        lax.fori_loop(0, n_exp, z_start, 0)
        lax.fori_loop(0, n_exp, z_wait, 0)

    def copies(r):
        t = base + r
        return (_row_copy(h_ref, r, xs_ref, _dest_row(c1_ref, offs_ref, t), sem.at[0]),
                _row_copy(h_ref, r, xs_ref, _dest_row(c2_ref, offs_ref, t), sem.at[1]))

    def issue(r, carry):
        a, b = copies(r)
        a.start()
        b.start()
        return carry

    def drain(r, carry):
        a, b = copies(r)
        a.wait()
        b.wait()
        return carry

    lax.fori_loop(0, tr, issue, 0)
    lax.fori_loop(0, tr, drain, 0)


def _dispatch(hp, code1, code2, offs, zero_block, zero_flag, *, rows, sub):
    n, dw = hp.shape
    tr = _tile(n, 256)
    grid_spec = pltpu.PrefetchScalarGridSpec(
        num_scalar_prefetch=5,
        grid=(n // tr,),
        in_specs=[pl.BlockSpec((tr, dw), lambda i, *_: (i, 0))],
        out_specs=pl.BlockSpec(memory_space=pl.ANY),
        scratch_shapes=[pltpu.VMEM((sub, dw), hp.dtype), pltpu.SemaphoreType.DMA((3,))],
    )
    return pl.pallas_call(
        functools.partial(_dispatch_kernel, n_exp=zero_block.shape[0]),
        out_shape=jax.ShapeDtypeStruct((rows, dw), hp.dtype),
        grid_spec=grid_spec,
        compiler_params=_params("arbitrary"),
        name="dispatch",
    )(code1, code2, offs, zero_block, zero_flag, hp)


def _pack_halves(lo, hi):
    lo_w = pltpu.bitcast(lo.astype(BF16).astype(F32), jnp.uint32) >> 16
    hi_w = pltpu.bitcast(hi.astype(BF16).astype(F32), jnp.uint32) & jnp.uint32(0xFFFF0000)
    return lo_w | hi_w


def _unpack_halves(w):
    return (pltpu.bitcast(w << 16, F32), pltpu.bitcast(w & jnp.uint32(0xFFFF0000), F32))


def _expert_kernel(se_ref, sb_ref, sn_ref, ns_ref, xs_ref, wg_ref, wu_ref, wdl_ref, wdh_ref, o_ref,
                   hid_ref, *, n_up, sub):
    del se_ref, sb_ref
    s, p = pl.program_id(0), pl.program_id(1)
    valid = s < ns_ref[0]
    n_sub = sn_ref[s]
    half = xs_ref.shape[1]
    fc = wg_ref.shape[1]

    def sub_rows(q):
        return pl.ds(pl.multiple_of(q * sub, sub), sub)

    @pl.when(jnp.logical_and(valid, p < n_up))
    def _():
        wg = wg_ref[...].astype(BF16)
        wu = wu_ref[...].astype(BF16)

        def body(q, carry):
            rows = sub_rows(q)
            x_lo, x_hi = _unpack_halves(xs_ref[rows, :])
            x_lo, x_hi = x_lo.astype(BF16), x_hi.astype(BF16)
            a = _dot(x_lo, wg[:half]) + _dot(x_hi, wg[half:])
            b = _dot(x_lo, wu[:half]) + _dot(x_hi, wu[half:])
            hid_ref[p, rows, :] = (a * _sigmoid(a) * b).astype(BF16)
            return carry

        lax.fori_loop(0, n_sub, body, 0)

    @pl.when(jnp.logical_and(valid, p >= n_up))
    def _():
        wl = wdl_ref[...].astype(BF16)
        wh = wdh_ref[...].astype(BF16)

        def body(q, carry):
            rows = sub_rows(q)
            y_lo = _dot(hid_ref[0, rows, :], wl[0:fc])
            y_hi = _dot(hid_ref[0, rows, :], wh[0:fc])
            for j in range(1, n_up):
                y_lo += _dot(hid_ref[j, rows, :], wl[j * fc:(j + 1) * fc])
                y_hi += _dot(hid_ref[j, rows, :], wh[j * fc:(j + 1) * fc])
            o_ref[rows, :] = _pack_halves(y_lo, y_hi)
            return carry

        def zero_body(q, carry):
            o_ref[sub_rows(q), :] = jnp.zeros((sub, o_ref.shape[1]), o_ref.dtype)
            return carry

        lax.fori_loop(0, n_sub, body, 0)
        lax.fori_loop(n_sub, xs_ref.shape[0] // sub, zero_body, 0)


def _experts(xs, w_gate, w_up, w_down, sup_expert, sup_block, sup_nsub, n_super, *, sub, sup):
    rows, dw = xs.shape
    _, d, f = w_gate.shape
    fc = _tile(f, 256)
    n_up = f // fc
    dc = _tile(dw, min(512, dw // 2))
    n_down = dw // dc
    n_phase = n_up + n_down

    def phase(s, p, ns):
        return jnp.where(s < ns[0], p, n_phase - 1)

    def x_map(s, p, se, sb, sn, ns):
        return (sb[s + (phase(s, p, ns) >= n_up).astype(jnp.int32)], 0)

    def up_map(first_down_phase):
        def index_map(s, p, se, sb, sn, ns):
            ph = phase(s, p, ns)
            ahead = ph >= n_up + first_down_phase
            return (se[s + ahead.astype(jnp.int32)], 0, jnp.where(ahead, 0, jnp.minimum(ph, n_up - 1)))
        return index_map

    def down_c(s, p, ns):
        return jnp.maximum(phase(s, p, ns) - n_up, 0)

    def down_map(col0):
        def index_map(s, p, se, sb, sn, ns):
            early = jnp.logical_and(phase(s, p, ns) < n_up - 1, s > 0)
            expert = se[jnp.where(early, s - 1, s)]
            return (expert, 0, col0 + jnp.where(early, n_down - 1, down_c(s, p, ns)))
        return index_map

    grid_spec = pltpu.PrefetchScalarGridSpec(
        num_scalar_prefetch=4,
        grid=(rows // sup, n_phase),
        in_specs=[pl.BlockSpec((sup, dw), x_map),
                  pl.BlockSpec((None, d, fc), up_map(0)),
                  pl.BlockSpec((None, d, fc), up_map(1)),
                  pl.BlockSpec((None, f, dc), down_map(0)),
                  pl.BlockSpec((None, f, dc), down_map(n_down))],
        out_specs=pl.BlockSpec((sup, dc), lambda s, p, se, sb, sn, ns: (sb[s], down_c(s, p, ns))),
        scratch_shapes=[pltpu.VMEM((n_up, sup, fc), BF16)],
    )
    return pl.pallas_call(
        functools.partial(_expert_kernel, n_up=n_up, sub=sub),
        out_shape=jax.ShapeDtypeStruct((rows, dw), jnp.uint32),
        grid_spec=grid_spec,
        compiler_params=_params("arbitrary", "arbitrary"),
        name="expert_mlp",
    )(sup_expert, sup_block, sup_nsub, n_super, xs, w_gate, w_up, w_down, w_down)


def _combine_kernel(c1_ref, c2_ref, offs_ref, x_ref, route_ref, g_ref, ys_ref, *rest, final):
    if final:
        y_ref, buf1, buf2, sem = rest
    else:
        x2_ref, hn_ref, buf1, buf2, sem = rest
    tr = x_ref.shape[0]
    base = pl.program_id(0) * tr

    def copies(r):
        t = base + r
        return (_row_copy(ys_ref, _dest_row(c1_ref, offs_ref, t), buf1, r, sem.at[0]),
                _row_copy(ys_ref, _dest_row(c2_ref, offs_ref, t), buf2, r, sem.at[1]))

    def issue(r, carry):
        a, b = copies(r)
        a.start()
        b.start()
        return carry

    def drain(r, carry):
        a, b = copies(r)
        a.wait()
        b.wait()
        return carry

    lax.fori_loop(0, tr, issue, 0)
    lax.fori_loop(0, tr, drain, 0)

    route = route_ref[...]
    w1, w2 = route[:, 2:3], route[:, 3:4]
    half = buf1.shape[1]
    d = 2 * half
    lo1, hi1 = _unpack_halves(buf1[...])
    lo2, hi2 = _unpack_halves(buf2[...])
    x2_lo = x_ref[:, :half] + w1 * lo1 + w2 * lo2
    x2_hi = x_ref[:, half:] + w1 * hi1 + w2 * hi2
    ms = (jnp.sum(x2_lo * x2_lo, axis=-1, keepdims=True) +
          jnp.sum(x2_hi * x2_hi, axis=-1, keepdims=True)) * (1.0 / d)
    inv = lax.rsqrt(ms + RMS_EPS)
    out_ref = y_ref if final else hn_ref
    if not final:
        x2_ref[:, :half] = x2_lo
        x2_ref[:, half:] = x2_hi
    out_ref[:, :half] = (x2_lo * inv * g_ref[:, :half]).astype(out_ref.dtype)
    out_ref[:, half:] = (x2_hi * inv * g_ref[:, half:]).astype(out_ref.dtype)


def _combine(x, route, g, ys, code1, code2, offs, *, final):
    n, d = x.shape
    tr = _tile(n, 256)
    row = lambda i, *_: (i, 0)
    if final:
        out_shape = jax.ShapeDtypeStruct((n, d), F32)
        out_specs = pl.BlockSpec((tr, d), row)
    else:
        out_shape = (jax.ShapeDtypeStruct((n, d), F32), jax.ShapeDtypeStruct((n, d), BF16))
        out_specs = (pl.BlockSpec((tr, d), row), pl.BlockSpec((tr, d), row))
    grid_spec = pltpu.PrefetchScalarGridSpec(
        num_scalar_prefetch=3,
        grid=(n // tr,),
        in_specs=[pl.BlockSpec((tr, d), row),
                  pl.BlockSpec((tr, ROUTE_COLS), row),
                  pl.BlockSpec((1, d), lambda i, *_: (0, 0)),
                  pl.BlockSpec(memory_space=pl.ANY)],
        out_specs=out_specs,
        scratch_shapes=[pltpu.VMEM((tr, d // 2), jnp.uint32), pltpu.VMEM((tr, d // 2), jnp.uint32),
                        pltpu.SemaphoreType.DMA((2,))],
    )
    return pl.pallas_call(
        functools.partial(_combine_kernel, final=final),
        out_shape=out_shape,
        grid_spec=grid_spec,
        compiler_params=_params("arbitrary"),
        name="combine_final" if final else "combine",
    )(code1, code2, offs, x, route, g.reshape(1, d), ys)


def _plan(route, counts, *, ge, sub, sup, max_super):
    i32 = jnp.int32
    e1, e2 = route[:, 0].astype(i32), route[:, 1].astype(i32)
    rank1, rank2 = route[:, 4].astype(i32), route[:, 5].astype(i32)
    cnt = counts[0, :ge].astype(i32)
    n_sub = (cnt + sub - 1) // sub
    n_sup = (cnt + sup - 1) // sup
    ends = jnp.cumsum(n_sup * sup)
    offs = ends - n_sup * sup
    code1 = e1 * (1 << ROUTE_CODE_SHIFT) + rank1
    code2 = e2 * (1 << ROUTE_CODE_SHIFT) + rank2
    sup_ends = jnp.cumsum(n_sup)
    n_super = sup_ends[-1]
    s = jnp.arange(max_super + 1, dtype=i32)
    se = jnp.minimum(jnp.sum((sup_ends[None, :] <= s[:, None]).astype(i32), axis=1), ge - 1)
    k = s - (sup_ends[se] - n_sup[se])
    sb = offs[se] // sup + k
    sn = jnp.clip(n_sub[se] - k * (sup // sub), 0, sup // sub)
    last = jnp.maximum(n_super - 1, 0)
    valid = s < n_super
    sup_expert = jnp.where(valid, se, se[last]).astype(i32)
    sup_block = jnp.where(valid, sb, sb[last]).astype(i32)
    sup_nsub = jnp.where(valid, sn, 0).astype(i32)
    zero_block = (offs // sub + cnt // sub).astype(i32)
    zero_flag = (cnt % sub != 0).astype(i32)
    return (code1, code2, offs.astype(i32), sup_expert, sup_block, sup_nsub,
            n_super.reshape(1).astype(i32), zero_block, zero_flag)


def kernel(x, norm1_g, w_in, lb_logits, hgrn_norm_g, w_pool, pool_scale, w_branch_a, w_branch_b,
           w_out, norm2_g, w_router_group, w_router_expert, w_gate, w_up, w_down, final_norm_g):
    batch, seq, d = x.shape
    n = batch * seq
    depth = w_in.shape[0]
    width = lb_logits.shape[1]
    pool_width = pool_scale.shape[1]
    n_groups = w_router_group.shape[-1]
    n_experts = w_gate.shape[2]
    ge = n_groups * n_experts
    d_expert = w_gate.shape[-1]
    assert n_groups + ge <= LANES and n < (1 << ROUTE_CODE_SHIFT)
    col_pool = 4 * width
    col_ga = col_pool + pool_width
    col_gb = col_ga + d

    sub = MOE_SUB_ROWS if n >= 4096 else 32
    sup = MOE_SUBS_PER_SUPER * sub
    max_super = (TOP_K * n + ge * (sup - 1)) // sup
    rows = max_super * sup

    wg_all = w_gate.reshape(depth * ge, d, d_expert)
    wu_all = w_up.reshape(depth * ge, d, d_expert)
    wd_all = w_down.reshape(depth * ge, d_expert, d)

    xf = x.reshape(n, d)
    h = _rmsnorm(xf, norm1_g[0], BF16)
    for l in range(depth):
        proj = _inproj(h, w_in, l)
        o_a = _hgrn(proj, lb_logits, hgrn_norm_g[l], layer=l, batch=batch, seq=seq, width=width)
        o_b = _pool(proj, w_pool, pool_scale, layer=l, batch=batch, seq=seq, col0=col_pool)
        merged = _merge(o_a, o_b, w_branch_a, w_branch_b, proj, layer=l, col_ga=col_ga, col_gb=col_gb)
        x1 = _outproj(merged, w_out, xf, l)

        w_router = jnp.concatenate([w_router_group[l], w_router_expert[l]], axis=1)
        w_router = jnp.pad(w_router, ((0, 0), (0, LANES - w_router.shape[1])))
        hp, route, counts = _router(x1, norm2_g[l], w_router, n_groups=n_groups, n_experts=n_experts)
        (code1, code2, offs, sup_expert, sup_block, sup_nsub, n_super, zero_block, zero_flag) = _plan(
            route, counts, ge=ge, sub=sub, sup=sup, max_super=max_super)
        xs = _dispatch(hp, code1, code2, offs, zero_block, zero_flag, rows=rows, sub=sub)
        ys = _experts(xs, wg_all, wu_all, wd_all, sup_expert + l * ge, sup_block, sup_nsub, n_super,
                      sub=sub, sup=sup)
        if l + 1 < depth:
            xf, h = _combine(x1, route, norm1_g[l + 1], ys, code1, code2, offs, final=False)
        else:
            out = _combine(x1, route, final_norm_g, ys, code1, code2, offs, final=True)
    return out.reshape(batch, seq, d)
```

```python
import functools

import jax
import jax.numpy as jnp
from jax import lax
from jax.experimental import pallas as pl
from jax.experimental.pallas import tpu as pltpu

RMS_EPS = 1e-6
MIN_FORGET = 1e-30
LOG2E = 1.4426950408889634
POOL_WINDOWS = (2, 4, 8, 16)
POOL_HALO = 16
HGRN_CHUNK = 64
HGRN_PAD = 8
HGRN_LEVEL_BLOCKS = (64, 32, 16, 8, 4)
TOP_K = 2
MOE_SUB_ROWS = 256
MOE_SUBS_PER_SUPER = 4
LANES = 128
ROUTE_COLS = 8
ROUTE_CODE_SHIFT = 20
DMA_ISSUE_UNROLL = 8
VMEM_LIMIT_BYTES = 56 * 1024 * 1024

F32 = jnp.float32
BF16 = jnp.bfloat16


def _params(*sem):
    return pltpu.CompilerParams(dimension_semantics=sem, vmem_limit_bytes=VMEM_LIMIT_BYTES)


def _tile(n, pref):
    t = min(n, pref)
    assert n % t == 0, (n, pref)
    return t


def _sigmoid(z):
    return 1.0 / (1.0 + jnp.exp(-z))


def _dot(a, b):
    return jnp.dot(a, b, preferred_element_type=F32)


def _dot_nt(a, b):
    return lax.dot_general(a, b, (((1,), (1,)), ((), ())), preferred_element_type=F32)


def _rms(x, g):
    return x * lax.rsqrt(jnp.mean(x * x, axis=-1, keepdims=True) + RMS_EPS) * g


def _rmsnorm_kernel(x_ref, g_ref, o_ref):
    o_ref[...] = _rms(x_ref[...], g_ref[...]).astype(o_ref.dtype)


def _rmsnorm(x, g, out_dtype):
    n, d = x.shape
    tr = _tile(n, 256)
    return pl.pallas_call(
        _rmsnorm_kernel,
        out_shape=jax.ShapeDtypeStruct((n, d), out_dtype),
        grid=(n // tr,),
        in_specs=[pl.BlockSpec((tr, d), lambda i: (i, 0)),
                  pl.BlockSpec((1, d), lambda i: (0, 0))],
        out_specs=pl.BlockSpec((tr, d), lambda i: (i, 0)),
        compiler_params=_params("parallel"),
        name="rmsnorm",
    )(x, g.reshape(1, d))


def _inproj_kernel(h_ref, w_ref, o_ref):
    o_ref[...] = _dot(h_ref[...], w_ref[...].astype(BF16))


def _inproj(h, w, layer):
    n, d = h.shape
    cols = w.shape[2]
    bm, bn = _tile(n, 1024), _tile(cols, 512)
    return pl.pallas_call(
        _inproj_kernel,
        out_shape=jax.ShapeDtypeStruct((n, cols), F32),
        grid=(n // bm, cols // bn),
        in_specs=[pl.BlockSpec((bm, d), lambda i, j: (i, 0)),
                  pl.BlockSpec((None, d, bn), lambda i, j: (layer, 0, j))],
        out_specs=pl.BlockSpec((bm, bn), lambda i, j: (i, j)),
        compiler_params=_params("parallel", "arbitrary"),
        name="inproj",
    )(h, w)


def _split3(x):
    hi = x.astype(BF16)
    r1 = x - hi.astype(F32)
    mid = r1.astype(BF16)
    lo = (r1 - mid.astype(F32)).astype(BF16)
    return hi, mid, lo


def _hgrn_kernel(lbl_ref, g_ref, q_ref, f_ref, i_ref, og_ref, o_ref,
                 st_ref, kp_ref, bp_ref, vp_ref, oc_ref, ac_ref, lm_ref, *, layer, n_chunks, n_heads):
    C, PAD = HGRN_CHUNK, HGRN_PAD
    LEVEL_BLOCKS = HGRN_LEVEL_BLOCKS
    dk = g_ref.shape[1]

    @pl.when(pl.program_id(2) == 0)
    def _():
        st_ref[...] = jnp.zeros_like(st_ref)

    zpad = jnp.zeros((PAD, dk), F32)
    for hh in range(n_heads):
        kp_ref[hh, 0:PAD, :] = zpad
        bp_ref[hh, 0:PAD, :] = zpad
        vp_ref[hh, 0:PAD, :] = zpad

    t_i = lax.broadcasted_iota(jnp.int32, (C, C), 0)
    s_i = lax.broadcasted_iota(jnp.int32, (C, C), 1)
    for lvl, blk in enumerate(LEVEL_BLOCKS):
        same = (t_i // blk) == (s_i // blk)
        take = same & ((t_i % blk) >= blk // 2) & ((s_i % blk) < blk // 2)
        lm_ref[lvl] = jnp.where(take, 1.0, 0.0)
    sub8 = lax.broadcasted_iota(jnp.int32, (8, dk), 0)

    lbl = lbl_ref[...]
    e = jnp.exp(lbl - jnp.max(lbl, axis=0, keepdims=True))
    p = e / jnp.sum(e, axis=0, keepdims=True)
    lb_all = jnp.sum(p[0:layer + 1], axis=0, keepdims=True) - p[0:1]
    gain = g_ref[...]

    tri = jnp.where(t_i >= s_i, 1.0, 0.0).astype(BF16)
    tri3 = jnp.concatenate([tri, tri, tri], axis=1)


    def gates(hh, r0):
        rows, cols = pl.ds(r0, C), slice(hh * dk, (hh + 1) * dk)
        lb = lb_all[:, cols]
        z = f_ref[rows, cols]
        f = lb + (1.0 - lb) * _sigmoid(z)
        lf = jnp.log(jnp.maximum(f, MIN_FORGET))
        k = (1.0 - lb) * _sigmoid(-z)
        hi, mid, lo = _split3(lf)
        b2 = _dot(tri3, jnp.concatenate([hi, mid, lo], axis=0)) * LOG2E
        kp_ref[hh, PAD:PAD + C, :] = k
        bp_ref[hh, PAD:PAD + C, :] = b2
        vp_ref[hh, PAD:PAD + C, :] = i_ref[rows, cols]

    def scores(hh, r0):
        rows, cols = pl.ds(r0, C), slice(hh * dk, (hh + 1) * dk)
        q = q_ref[rows, cols]
        k, b2 = kp_ref[hh, PAD:PAD + C, :], bp_ref[hh, PAD:PAD + C, :]
        oc_ref[hh] = _dot_nt((q * jnp.exp2(b2)).astype(BF16), st_ref[hh].astype(BF16))

        a = jnp.zeros((C, C), F32)
        for lvl, blk in enumerate(LEVEL_BLOCKS):
            hb_ = blk // 2
            if hb_ >= 8:
                xs_, es_ = [], []
                for b0 in range(0, C, blk):
                    m = bp_ref[hh, PAD + b0 + hb_ - 1:PAD + b0 + hb_, :]
                    xs_ += [k[b0:b0 + hb_], q[b0 + hb_:b0 + blk]]
                    es_ += [m - b2[b0:b0 + hb_], b2[b0 + hb_:b0 + blk] - m]
                x = jnp.concatenate(xs_, axis=0)
                e = jnp.minimum(jnp.concatenate(es_, axis=0), 0.0)
            else:
                up8 = (sub8 & (blk - 1)) >= hb_
                xs_, es_ = [], []
                for g0 in range(0, C, 8):
                    rows_m = [bp_ref[hh, PAD + b0 + hb_ - 1:PAD + b0 + hb_, :] for b0 in range(g0, g0 + 8, blk)]
                    mg = jnp.broadcast_to(rows_m[-1], (8, dk))
                    for j in range(len(rows_m) - 2, -1, -1):
                        mg = jnp.where(sub8 < (j + 1) * blk, rows_m[j], mg)
                    xs_.append(jnp.where(up8, q[g0:g0 + 8], k[g0:g0 + 8]))
                    es_.append(-jnp.abs(b2[g0:g0 + 8] - mg))
                x = jnp.concatenate(xs_, axis=0)
                e = jnp.concatenate(es_, axis=0)
            y = (x * jnp.exp2(e)).astype(BF16)
            a = a + _dot_nt(y, y) * lm_ref[lvl]
        ac_ref[hh] = a.astype(BF16)

    def outputs(hh, r0):
        rows, cols = pl.ds(r0, C), slice(hh * dk, (hh + 1) * dk)
        q, v = q_ref[rows, cols], i_ref[rows, cols]
        k, b2 = kp_ref[hh, PAD:PAD + C, :], bp_ref[hh, PAD:PAD + C, :]
        o = oc_ref[hh] + _dot(ac_ref[hh], v.astype(BF16))
        k1, b1, v1 = kp_ref[hh, PAD - 1:PAD - 1 + C, :], bp_ref[hh, PAD - 1:PAD - 1 + C, :], vp_ref[hh, PAD - 1:PAD - 1 + C, :]
        a1 = jnp.sum(q * k1 * jnp.exp2(jnp.minimum(b2 - b1, 0.0)), axis=-1, keepdims=True)
        a0 = jnp.sum(q * k, axis=-1, keepdims=True)
        odd8 = (sub8 & 1) == 1
        a1 = jnp.concatenate([jnp.where(odd8, a1[g0:g0 + 8], 0.0) for g0 in range(0, C, 8)], axis=0)
        oc_ref[hh] = o + a1 * v1 + a0 * v

    def finish(hh, r0):
        rows, cols = pl.ds(r0, C), slice(hh * dk, (hh + 1) * dk)
        k, b2, v = kp_ref[hh, PAD:PAD + C, :], bp_ref[hh, PAD:PAD + C, :], i_ref[rows, cols]
        b_last = b2[C - 1:C, :]
        khat = k * jnp.exp2(b_last - b2)
        st_ref[hh] = st_ref[hh] * jnp.exp2(b_last) + _dot(v.T.astype(BF16), khat.astype(BF16))
        og = og_ref[rows, cols]
        y = _rms(oc_ref[hh], gain) * (og * _sigmoid(og))
        o_ref[rows, cols] = y.astype(o_ref.dtype)

    def chunk(c, carry):
        r0 = pl.multiple_of(c * C, C)
        for phase in (gates, scores, outputs, finish):
            for hh in range(n_heads):
                phase(hh, r0)
        return carry

    lax.fori_loop(0, n_chunks, chunk, 0)


def _hgrn(proj, lb_logits, gain, *, layer, batch, seq, width):
    n = proj.shape[0]
    dk = gain.shape[-1]
    heads = width // dk
    hb = _tile(heads, 16)
    hg = heads // hb
    t = _tile(seq, 512)
    ns = seq // t
    n_layers = lb_logits.shape[0]

    def col(seg):
        return pl.BlockSpec((t, hb * dk), lambda b, h, s, seg=seg: (b * ns + s, seg * hg + h))

    kern = functools.partial(_hgrn_kernel, layer=layer, n_chunks=t // HGRN_CHUNK, n_heads=hb)
    buf = pltpu.VMEM((hb, HGRN_PAD + HGRN_CHUNK, dk), F32)
    cbuf = pltpu.VMEM((hb, HGRN_CHUNK, dk), F32)
    masks = pltpu.VMEM((len(HGRN_LEVEL_BLOCKS), HGRN_CHUNK, HGRN_CHUNK), F32)
    return pl.pallas_call(
        kern,
        out_shape=jax.ShapeDtypeStruct((n, width), BF16),
        grid=(batch, hg, ns),
        in_specs=[pl.BlockSpec((n_layers, hb * dk), lambda b, h, s: (0, h)),
                  pl.BlockSpec((1, dk), lambda b, h, s: (0, 0)),
                  col(0), col(1), col(2), col(3)],
        out_specs=pl.BlockSpec((t, hb * dk), lambda b, h, s: (b * ns + s, h)),
        scratch_shapes=[pltpu.VMEM((hb, dk, dk), F32), buf, buf, buf, cbuf,
                        pltpu.VMEM((hb, HGRN_CHUNK, HGRN_CHUNK), BF16), masks],
        compiler_params=_params("parallel", "parallel", "arbitrary"),
        name="hgrn2",
    )(lb_logits, gain.reshape(1, dk), proj, proj, proj, proj)


def _pool_kernel(u_ref, halo_ref, w_ref, sc_ref, o_ref, *, windows):
    g = pl.program_id(2)
    s = pl.program_id(1)
    t = u_ref.shape[0]
    u = u_ref[...]
    win = jnp.int32(windows[0])
    for gi in range(1, len(windows)):
        win = jnp.where(g == gi, jnp.int32(windows[gi]), win)
    r_i = lax.broadcasted_iota(jnp.int32, (t, POOL_HALO + t), 0)
    c_i = lax.broadcasted_iota(jnp.int32, (t, POOL_HALO + t), 1) - POOL_HALO
    lag = r_i - c_i
    inside = (lag >= 0) & (lag < win) & (c_i + s * t >= 0)
    band = jnp.where(inside, 1.0, 0.0).astype(BF16)
    ext = jnp.concatenate([halo_ref[...].astype(BF16), u.astype(BF16)], axis=0)
    wsum = _dot(band, ext)
    pos = (lax.broadcasted_iota(jnp.int32, (t, 1), 0) + s * t + 1).astype(F32)
    pooled = wsum / jnp.minimum(pos, win.astype(F32)) - u
    mixed = _dot(pooled.astype(BF16), w_ref[...].astype(BF16))
    o_ref[...] = (mixed * sc_ref[...]).astype(o_ref.dtype)


def _pool(proj, w_pool, pool_scale, *, layer, batch, seq, col0):
    n = proj.shape[0]
    _, groups, cg, _ = w_pool.shape
    t = _tile(seq, 512)
    ns = seq // t
    hb = t // POOL_HALO
    assert col0 % cg == 0 and groups == len(POOL_WINDOWS)
    cb = col0 // cg
    kern = functools.partial(_pool_kernel, windows=POOL_WINDOWS)
    return pl.pallas_call(
        kern,
        out_shape=jax.ShapeDtypeStruct((n, groups * cg), BF16),
        grid=(batch, ns, groups),
        in_specs=[pl.BlockSpec((t, cg), lambda b, s, g: (b * ns + s, cb + g)),
                  pl.BlockSpec((POOL_HALO, cg),
                               lambda b, s, g: (jnp.maximum((b * ns + s) * hb - 1, 0), cb + g)),
                  pl.BlockSpec((None, None, cg, cg), lambda b, s, g: (layer, g, 0, 0)),
                  pl.BlockSpec((1, cg), lambda b, s, g: (0, g))],
        out_specs=pl.BlockSpec((t, cg), lambda b, s, g: (b * ns + s, g)),
        compiler_params=_params("parallel", "parallel", "arbitrary"),
        name="pool_mixer",
    )(proj, proj, w_pool, pool_scale[layer].reshape(1, groups * cg))


def _merge_kernel(oa_ref, ob_ref, wa_ref, wb_ref, ga_ref, gb_ref, o_ref):
    a = _dot(oa_ref[...], wa_ref[...].astype(BF16))
    b = _dot(ob_ref[...], wb_ref[...].astype(BF16))
    o_ref[...] = (_sigmoid(ga_ref[...]) * a + _sigmoid(gb_ref[...]) * b).astype(o_ref.dtype)


def _merge(o_a, o_b, w_a, w_b, proj, *, layer, col_ga, col_gb):
    n, wa = o_a.shape
    wb = o_b.shape[1]
    d = w_a.shape[2]
    bm, bn = _tile(n, 1024), _tile(d, 512)
    assert col_ga % bn == 0 and col_gb % bn == 0
    ca, cb = col_ga // bn, col_gb // bn
    return pl.pallas_call(
        _merge_kernel,
        out_shape=jax.ShapeDtypeStruct((n, d), BF16),
        grid=(n // bm, d // bn),
        in_specs=[pl.BlockSpec((bm, wa), lambda i, j: (i, 0)),
                  pl.BlockSpec((bm, wb), lambda i, j: (i, 0)),
                  pl.BlockSpec((None, wa, bn), lambda i, j: (layer, 0, j)),
                  pl.BlockSpec((None, wb, bn), lambda i, j: (layer, 0, j)),
                  pl.BlockSpec((bm, bn), lambda i, j: (i, ca + j)),
                  pl.BlockSpec((bm, bn), lambda i, j: (i, cb + j))],
        out_specs=pl.BlockSpec((bm, bn), lambda i, j: (i, j)),
        compiler_params=_params("parallel", "arbitrary"),
        name="gated_merge",
    )(o_a, o_b, w_a, w_b, proj, proj)


def _outproj_kernel(m_ref, w_ref, x_ref, o_ref):
    o_ref[...] = x_ref[...] + _dot(m_ref[...], w_ref[...].astype(BF16))


def _outproj(merged, w, x, layer):
    n, d = merged.shape
    dn = w.shape[2]
    bm, bn = _tile(n, 1024), _tile(dn, 512)
    return pl.pallas_call(
        _outproj_kernel,
        out_shape=jax.ShapeDtypeStruct((n, dn), F32),
        grid=(n // bm, dn // bn),
        in_specs=[pl.BlockSpec((bm, d), lambda i, j: (i, 0)),
                  pl.BlockSpec((None, d, bn), lambda i, j: (layer, 0, j)),
                  pl.BlockSpec((bm, bn), lambda i, j: (i, j))],
        out_specs=pl.BlockSpec((bm, bn), lambda i, j: (i, j)),
        compiler_params=_params("parallel", "arbitrary"),
        name="outproj_residual",
    )(merged, w, x)


def _router_kernel(x_ref, g_ref, wr_ref, hp_ref, route_ref, cnt_ref, run_ref, *, n_groups, n_experts):
    tr, d = x_ref.shape
    half = d // 2

    @pl.when(pl.program_id(0) == 0)
    def _():
        run_ref[...] = jnp.zeros_like(run_ref)

    h = _rms(x_ref[...], g_ref[...])
    hb = h.astype(BF16)
    hp_ref[...] = _pack_halves(h[:, :half], h[:, half:])

    h_lo = (h - hb.astype(F32)).astype(BF16)
    wr = wr_ref[...]
    w_hi = wr.astype(BF16)
    w_lo = (wr - w_hi.astype(F32)).astype(BF16)
    logits = _dot(hb, w_hi) + _dot(hb, w_lo) + _dot(h_lo, w_hi)

    lane = lax.broadcasted_iota(jnp.int32, (tr, LANES), 1).astype(F32)
    ninf = jnp.float32(-jnp.inf)
    big = jnp.float32(1e9)
    is_g = lane < n_groups
    gl = jnp.where(is_g, logits, ninf)
    gmax = jnp.max(gl, axis=-1, keepdims=True)
    gidx = jnp.min(jnp.where(gl == gmax, lane, big), axis=-1, keepdims=True)
    g_w = 1.0 / jnp.sum(jnp.where(is_g, jnp.exp(gl - gmax), 0.0), axis=-1, keepdims=True)

    e_lo = n_groups + gidx * n_experts
    el = jnp.where(lane >= e_lo, jnp.where(lane < e_lo + n_experts, logits, ninf), ninf)
    v1 = jnp.max(el, axis=-1, keepdims=True)
    i1 = jnp.min(jnp.where(el == v1, lane, big), axis=-1, keepdims=True)
    el2 = jnp.where(lane == i1, ninf, el)
    v2 = jnp.max(el2, axis=-1, keepdims=True)
    i2 = jnp.min(jnp.where(el2 == v2, lane, big), axis=-1, keepdims=True)
    tt = jnp.exp(v2 - v1)
    w1 = g_w / (1.0 + tt)
    w2 = g_w * tt / (1.0 + tt)
    e1 = i1 - n_groups
    e2 = i2 - n_groups

    oh1 = lane == e1
    oh2 = lane == e2
    cnt = jnp.where(oh1, 1.0, 0.0) + jnp.where(oh2, 1.0, 0.0)
    r_i = lax.broadcasted_iota(jnp.int32, (tr, tr), 0)
    c_i = lax.broadcasted_iota(jnp.int32, (tr, tr), 1)
    strict = jnp.where(r_i > c_i, 1.0, 0.0).astype(BF16)
    before = _dot(strict, cnt.astype(BF16)) + run_ref[0:1, :]
    rank1 = jnp.sum(jnp.where(oh1, before, 0.0), axis=-1, keepdims=True)
    rank2 = jnp.sum(jnp.where(oh2, before, 0.0), axis=-1, keepdims=True)
    run = run_ref[0:1, :] + jnp.sum(cnt, axis=0, keepdims=True)
    run_ref[...] = jnp.broadcast_to(run, run_ref.shape)
    cnt_ref[...] = jnp.broadcast_to(run, cnt_ref.shape)

    out = jnp.zeros((tr, LANES), F32)
    for idx, val in enumerate((e1, e2, w1, w2, rank1, rank2)):
        out = jnp.where(lane == idx, val, out)
    route_ref[...] = out[:, :ROUTE_COLS]


def _router(x, g, w_router, *, n_groups, n_experts):
    n, d = x.shape
    tr = _tile(n, 256)
    kern = functools.partial(_router_kernel, n_groups=n_groups, n_experts=n_experts)
    return pl.pallas_call(
        kern,
        out_shape=(jax.ShapeDtypeStruct((n, d // 2), jnp.uint32),
                   jax.ShapeDtypeStruct((n, ROUTE_COLS), F32),
                   jax.ShapeDtypeStruct((8, LANES), F32)),
        grid=(n // tr,),
        in_specs=[pl.BlockSpec((tr, d), lambda i: (i, 0)),
                  pl.BlockSpec((1, d), lambda i: (0, 0)),
                  pl.BlockSpec((d, LANES), lambda i: (0, 0))],
        out_specs=(pl.BlockSpec((tr, d // 2), lambda i: (i, 0)),
                   pl.BlockSpec((tr, ROUTE_COLS), lambda i: (i, 0)),
                   pl.BlockSpec((8, LANES), lambda i: (0, 0))),
        scratch_shapes=[pltpu.VMEM((8, LANES), F32)],
        compiler_params=_params("arbitrary"),
        name="norm_router",
    )(x, g.reshape(1, d), w_router)


def _row_copy(src_ref, src_row, dst_ref, dst_row, sem):
    return pltpu.make_async_copy(src_ref.at[pl.ds(src_row, 1)], dst_ref.at[pl.ds(dst_row, 1)], sem)


def _start_then_wait_rows(copies, n_rows):
    unroll = DMA_ISSUE_UNROLL

    def issue(g, carry):
        for u in range(unroll):
            for c in copies(g * unroll + u):
                c.start()
        return carry

    lax.fori_loop(0, n_rows // unroll, issue, 0)
    first = copies(0)

    def drain(g, carry):
        for _ in range(unroll):
            for c in first:
                c.wait()
        return carry

    lax.fori_loop(0, n_rows // unroll, drain, 0)


def _dest_row(code_ref, offs_ref, t):
    code = code_ref[t]
    return offs_ref[code >> ROUTE_CODE_SHIFT] + (code & ((1 << ROUTE_CODE_SHIFT) - 1))


def _dispatch_kernel(c1_ref, c2_ref, offs_ref, zb_ref, zf_ref, h_ref, xs_ref, zero_ref, sem, *, n_exp):
    tr = h_ref.shape[0]
    sub = zero_ref.shape[0]
    base = pl.program_id(0) * tr

    def zero_copy(e):
        dst = xs_ref.at[pl.ds(pl.multiple_of(zb_ref[e] * sub, sub), sub)]
        return pltpu.make_async_copy(zero_ref, dst, sem.at[2])

    @pl.when(pl.program_id(0) == 0)
    def _():
        zero_ref[...] = jnp.zeros_like(zero_ref)

        def z_start(e, carry):
            @pl.when(zf_ref[e] != 0)
            def _():
                zero_copy(e).start()
            return carry

        def z_wait(e, carry):
            @pl.when(zf_ref[e] != 0)
            def _():
                zero_copy(e).wait()
            return carry

        lax.fori_loop(0, n_exp, z_start, 0)
        lax.fori_loop(0, n_exp, z_wait, 0)

    def copies(r):
        t = base + r
        return (_row_copy(h_ref, r, xs_ref, _dest_row(c1_ref, offs_ref, t), sem.at[0]),
                _row_copy(h_ref, r, xs_ref, _dest_row(c2_ref, offs_ref, t), sem.at[1]))

    _start_then_wait_rows(copies, tr)


def _dispatch(hp, code1, code2, offs, zero_block, zero_flag, *, rows, sub):
    n, dw = hp.shape
    tr = _tile(n, 256)
    grid_spec = pltpu.PrefetchScalarGridSpec(
        num_scalar_prefetch=5,
        grid=(n // tr,),
        in_specs=[pl.BlockSpec((tr, dw), lambda i, *_: (i, 0))],
        out_specs=pl.BlockSpec(memory_space=pl.ANY),
        scratch_shapes=[pltpu.VMEM((sub, dw), hp.dtype), pltpu.SemaphoreType.DMA((3,))],
    )
    return pl.pallas_call(
        functools.partial(_dispatch_kernel, n_exp=zero_block.shape[0]),
        out_shape=jax.ShapeDtypeStruct((rows, dw), hp.dtype),
        grid_spec=grid_spec,
        compiler_params=_params("arbitrary"),
        name="dispatch",
    )(code1, code2, offs, zero_block, zero_flag, hp)


def _pack_halves(lo, hi):
    lo_w = pltpu.bitcast(lo.astype(BF16).astype(F32), jnp.uint32) >> 16
    hi_w = pltpu.bitcast(hi.astype(BF16).astype(F32), jnp.uint32) & jnp.uint32(0xFFFF0000)
    return lo_w | hi_w


def _unpack_halves(w):
    return (pltpu.bitcast(w << 16, F32), pltpu.bitcast(w & jnp.uint32(0xFFFF0000), F32))


def _expert_kernel(se_ref, sb_ref, sn_ref, ns_ref, xs_ref, wg_ref, wu_ref, wdl_ref, wdh_ref, o_ref,
                   hid_ref, *, n_up, sub):
    del se_ref, sb_ref
    s, p = pl.program_id(0), pl.program_id(1)
    valid = s < ns_ref[0]
    n_sub = sn_ref[s]
    half = xs_ref.shape[1]
    fc = wg_ref.shape[1]

    def sub_rows(q):
        return pl.ds(pl.multiple_of(q * sub, sub), sub)

    @pl.when(jnp.logical_and(valid, p < n_up))
    def _():
        wg = wg_ref[...].astype(BF16)
        wu = wu_ref[...].astype(BF16)

        def body(q, carry):
            rows = sub_rows(q)
            x_lo, x_hi = _unpack_halves(xs_ref[rows, :])
            x_lo, x_hi = x_lo.astype(BF16), x_hi.astype(BF16)
            a = _dot(x_lo, wg[:half]) + _dot(x_hi, wg[half:])
            b = _dot(x_lo, wu[:half]) + _dot(x_hi, wu[half:])
            hid_ref[p, rows, :] = (a * _sigmoid(a) * b).astype(BF16)
            return carry

        lax.fori_loop(0, n_sub, body, 0)

    @pl.when(jnp.logical_and(valid, p >= n_up))
    def _():
        wl = wdl_ref[...].astype(BF16)
        wh = wdh_ref[...].astype(BF16)

        def body(q, carry):
            rows = sub_rows(q)
            y_lo = _dot(hid_ref[0, rows, :], wl[0:fc])
            y_hi = _dot(hid_ref[0, rows, :], wh[0:fc])
            for j in range(1, n_up):
                y_lo += _dot(hid_ref[j, rows, :], wl[j * fc:(j + 1) * fc])
                y_hi += _dot(hid_ref[j, rows, :], wh[j * fc:(j + 1) * fc])
            o_ref[rows, :] = _pack_halves(y_lo, y_hi)
            return carry

        def zero_body(q, carry):
            o_ref[sub_rows(q), :] = jnp.zeros((sub, o_ref.shape[1]), o_ref.dtype)
            return carry

        lax.fori_loop(0, n_sub, body, 0)
        lax.fori_loop(n_sub, xs_ref.shape[0] // sub, zero_body, 0)


def _experts(xs, w_gate, w_up, w_down, sup_expert, sup_block, sup_nsub, n_super, *, sub, sup):
    rows, dw = xs.shape
    _, d, f = w_gate.shape
    fc = _tile(f, 256)
    n_up = f // fc
    dc = _tile(dw, min(512, dw // 2))
    n_down = dw // dc
    n_phase = n_up + n_down

    def phase(s, p, ns):
        return jnp.where(s < ns[0], p, n_phase - 1)

    def x_map(s, p, se, sb, sn, ns):
        return (sb[s + (phase(s, p, ns) >= n_up).astype(jnp.int32)], 0)

    def up_map(first_down_phase):
        def index_map(s, p, se, sb, sn, ns):
            ph = phase(s, p, ns)
            ahead = ph >= n_up + first_down_phase
            return (se[s + ahead.astype(jnp.int32)], 0, jnp.where(ahead, 0, jnp.minimum(ph, n_up - 1)))
        return index_map

    def down_c(s, p, ns):
        return jnp.maximum(phase(s, p, ns) - n_up, 0)

    def down_map(col0):
        def index_map(s, p, se, sb, sn, ns):
            early = jnp.logical_and(phase(s, p, ns) < n_up - 1, s > 0)
            expert = se[jnp.where(early, s - 1, s)]
            return (expert, 0, col0 + jnp.where(early, n_down - 1, down_c(s, p, ns)))
        return index_map

    grid_spec = pltpu.PrefetchScalarGridSpec(
        num_scalar_prefetch=4,
        grid=(rows // sup, n_phase),
        in_specs=[pl.BlockSpec((sup, dw), x_map),
                  pl.BlockSpec((None, d, fc), up_map(0)),
                  pl.BlockSpec((None, d, fc), up_map(1)),
                  pl.BlockSpec((None, f, dc), down_map(0)),
                  pl.BlockSpec((None, f, dc), down_map(n_down))],
        out_specs=pl.BlockSpec((sup, dc), lambda s, p, se, sb, sn, ns: (sb[s], down_c(s, p, ns))),
        scratch_shapes=[pltpu.VMEM((n_up, sup, fc), BF16)],
    )
    return pl.pallas_call(
        functools.partial(_expert_kernel, n_up=n_up, sub=sub),
        out_shape=jax.ShapeDtypeStruct((rows, dw), jnp.uint32),
        grid_spec=grid_spec,
        compiler_params=_params("arbitrary", "arbitrary"),
        name="expert_mlp",
    )(sup_expert, sup_block, sup_nsub, n_super, xs, w_gate, w_up, w_down, w_down)


def _combine_kernel(c1_ref, c2_ref, offs_ref, x_ref, route_ref, g_ref, ys_ref, *rest, final):
    if final:
        y_ref, buf1, buf2, sem = rest
    else:
        x2_ref, hn_ref, buf1, buf2, sem = rest
    tr = x_ref.shape[0]
    base = pl.program_id(0) * tr

    def copies(r):
        t = base + r
        return (_row_copy(ys_ref, _dest_row(c1_ref, offs_ref, t), buf1, r, sem.at[0]),
                _row_copy(ys_ref, _dest_row(c2_ref, offs_ref, t), buf2, r, sem.at[1]))

    _start_then_wait_rows(copies, tr)

    route = route_ref[...]
    w1, w2 = route[:, 2:3], route[:, 3:4]
    half = buf1.shape[1]
    d = 2 * half
    lo1, hi1 = _unpack_halves(buf1[...])
    lo2, hi2 = _unpack_halves(buf2[...])
    x2_lo = x_ref[:, :half] + w1 * lo1 + w2 * lo2
    x2_hi = x_ref[:, half:] + w1 * hi1 + w2 * hi2
    ms = (jnp.sum(x2_lo * x2_lo, axis=-1, keepdims=True) +
          jnp.sum(x2_hi * x2_hi, axis=-1, keepdims=True)) * (1.0 / d)
    inv = lax.rsqrt(ms + RMS_EPS)
    out_ref = y_ref if final else hn_ref
    if not final:
        x2_ref[:, :half] = x2_lo
        x2_ref[:, half:] = x2_hi
    out_ref[:, :half] = (x2_lo * inv * g_ref[:, :half]).astype(out_ref.dtype)
    out_ref[:, half:] = (x2_hi * inv * g_ref[:, half:]).astype(out_ref.dtype)


def _combine(x, route, g, ys, code1, code2, offs, *, final):
    n, d = x.shape
    tr = _tile(n, 256)
    row = lambda i, *_: (i, 0)
    if final:
        out_shape = jax.ShapeDtypeStruct((n, d), F32)
        out_specs = pl.BlockSpec((tr, d), row)
    else:
        out_shape = (jax.ShapeDtypeStruct((n, d), F32), jax.ShapeDtypeStruct((n, d), BF16))
        out_specs = (pl.BlockSpec((tr, d), row), pl.BlockSpec((tr, d), row))
    grid_spec = pltpu.PrefetchScalarGridSpec(
        num_scalar_prefetch=3,
        grid=(n // tr,),
        in_specs=[pl.BlockSpec((tr, d), row),
                  pl.BlockSpec((tr, ROUTE_COLS), row),
                  pl.BlockSpec((1, d), lambda i, *_: (0, 0)),
                  pl.BlockSpec(memory_space=pl.ANY)],
        out_specs=out_specs,
        scratch_shapes=[pltpu.VMEM((tr, d // 2), jnp.uint32), pltpu.VMEM((tr, d // 2), jnp.uint32),
                        pltpu.SemaphoreType.DMA((2,))],
    )
    return pl.pallas_call(
        functools.partial(_combine_kernel, final=final),
        out_shape=out_shape,
        grid_spec=grid_spec,
        compiler_params=_params("arbitrary"),
        name="combine_final" if final else "combine",
    )(code1, code2, offs, x, route, g.reshape(1, d), ys)


def _plan(route, counts, *, ge, sub, sup, max_super):
    i32 = jnp.int32
    e1, e2 = route[:, 0].astype(i32), route[:, 1].astype(i32)
    rank1, rank2 = route[:, 4].astype(i32), route[:, 5].astype(i32)
    cnt = counts[0, :ge].astype(i32)
    n_sub = (cnt + sub - 1) // sub
    n_sup = (cnt + sup - 1) // sup
    ends = jnp.cumsum(n_sup * sup)
    offs = ends - n_sup * sup
    code1 = e1 * (1 << ROUTE_CODE_SHIFT) + rank1
    code2 = e2 * (1 << ROUTE_CODE_SHIFT) + rank2
    sup_ends = jnp.cumsum(n_sup)
    n_super = sup_ends[-1]
    s = jnp.arange(max_super + 1, dtype=i32)
    se = jnp.minimum(jnp.sum((sup_ends[None, :] <= s[:, None]).astype(i32), axis=1), ge - 1)
    k = s - (sup_ends[se] - n_sup[se])
    sb = offs[se] // sup + k
    sn = jnp.clip(n_sub[se] - k * (sup // sub), 0, sup // sub)
    last = jnp.maximum(n_super - 1, 0)
    valid = s < n_super
    sup_expert = jnp.where(valid, se, se[last]).astype(i32)
    sup_block = jnp.where(valid, sb, sb[last]).astype(i32)
    sup_nsub = jnp.where(valid, sn, 0).astype(i32)
    zero_block = (offs // sub + cnt // sub).astype(i32)
    zero_flag = (cnt % sub != 0).astype(i32)
    return (code1, code2, offs.astype(i32), sup_expert, sup_block, sup_nsub,
            n_super.reshape(1).astype(i32), zero_block, zero_flag)


def kernel(x, norm1_g, w_in, lb_logits, hgrn_norm_g, w_pool, pool_scale, w_branch_a, w_branch_b,
           w_out, norm2_g, w_router_group, w_router_expert, w_gate, w_up, w_down, final_norm_g):
    batch, seq, d = x.shape
    n = batch * seq
    depth = w_in.shape[0]
    width = lb_logits.shape[1]
    pool_width = pool_scale.shape[1]
    n_groups = w_router_group.shape[-1]
    n_experts = w_gate.shape[2]
    ge = n_groups * n_experts
    d_expert = w_gate.shape[-1]
    assert n_groups + ge <= LANES and n < (1 << ROUTE_CODE_SHIFT)
    col_pool = 4 * width
    col_ga = col_pool + pool_width
    col_gb = col_ga + d

    sub = MOE_SUB_ROWS if n >= 4096 else 32
    sup = MOE_SUBS_PER_SUPER * sub
    max_super = (TOP_K * n + ge * (sup - 1)) // sup
    rows = max_super * sup

    wg_all = w_gate.reshape(depth * ge, d, d_expert)
    wu_all = w_up.reshape(depth * ge, d, d_expert)
    wd_all = w_down.reshape(depth * ge, d_expert, d)

    xf = x.reshape(n, d)
    h = _rmsnorm(xf, norm1_g[0], BF16)
    for l in range(depth):
        proj = _inproj(h, w_in, l)
        o_a = _hgrn(proj, lb_logits, hgrn_norm_g[l], layer=l, batch=batch, seq=seq, width=width)
        o_b = _pool(proj, w_pool, pool_scale, layer=l, batch=batch, seq=seq, col0=col_pool)
        merged = _merge(o_a, o_b, w_branch_a, w_branch_b, proj, layer=l, col_ga=col_ga, col_gb=col_gb)
        x1 = _outproj(merged, w_out, xf, l)

        w_router = jnp.concatenate([w_router_group[l], w_router_expert[l]], axis=1)
        w_router = jnp.pad(w_router, ((0, 0), (0, LANES - w_router.shape[1])))
        hp, route, counts = _router(x1, norm2_g[l], w_router, n_groups=n_groups, n_experts=n_experts)
        (code1, code2, offs, sup_expert, sup_block, sup_nsub, n_super, zero_block, zero_flag) = _plan(
            route, counts, ge=ge, sub=sub, sup=sup, max_super=max_super)
        xs = _dispatch(hp, code1, code2, offs, zero_block, zero_flag, rows=rows, sub=sub)
        ys = _experts(xs, wg_all, wu_all, wd_all, sup_expert + l * ge, sup_block, sup_nsub, n_super,
                      sub=sub, sup=sup)
        if l + 1 < depth:
            xf, h = _combine(x1, route, norm1_g[l + 1], ys, code1, code2, offs, final=False)
        else:
            out = _combine(x1, route, final_norm_g, ys, code1, code2, offs, final=True)
    return out.reshape(batch, seq, d)
```

```python
import functools

import jax
import jax.numpy as jnp
from jax import lax
from jax.experimental import pallas as pl
from jax.experimental.pallas import tpu as pltpu

RMS_EPS = 1e-6
MIN_FORGET = 1e-30
LOG2E = 1.4426950408889634
POOL_WINDOWS = (2, 4, 8, 16)
POOL_HALO = 16
HGRN_CHUNK = 64
HGRN_PAD = 8
HGRN_LEVEL_BLOCKS = (64, 32, 16, 8, 4)
TOP_K = 2
MOE_SUB_ROWS = 256
MOE_HALVES_PER_SUPER = 5
LANES = 128
ROUTE_COLS = 8
ROUTE_CODE_SHIFT = 20
DMA_ISSUE_UNROLL = 8
COMBINE_PARTS = 2
VMEM_LIMIT_BYTES = 56 * 1024 * 1024

F32 = jnp.float32
BF16 = jnp.bfloat16


def _params(*sem):
    return pltpu.CompilerParams(dimension_semantics=sem, vmem_limit_bytes=VMEM_LIMIT_BYTES)


def _tile(n, pref):
    t = min(n, pref)
    assert n % t == 0, (n, pref)
    return t


def _sigmoid(z):
    return 1.0 / (1.0 + jnp.exp(-z))


def _dot(a, b):
    return jnp.dot(a, b, preferred_element_type=F32)


def _dot_nt(a, b):
    return lax.dot_general(a, b, (((1,), (1,)), ((), ())), preferred_element_type=F32)


def _rms(x, g):
    return x * lax.rsqrt(jnp.mean(x * x, axis=-1, keepdims=True) + RMS_EPS) * g


def _rmsnorm_kernel(x_ref, g_ref, o_ref):
    o_ref[...] = _rms(x_ref[...], g_ref[...]).astype(o_ref.dtype)


def _rmsnorm(x, g, out_dtype):
    n, d = x.shape
    tr = _tile(n, 256)
    return pl.pallas_call(
        _rmsnorm_kernel,
        out_shape=jax.ShapeDtypeStruct((n, d), out_dtype),
        grid=(n // tr,),
        in_specs=[pl.BlockSpec((tr, d), lambda i: (i, 0)),
                  pl.BlockSpec((1, d), lambda i: (0, 0))],
        out_specs=pl.BlockSpec((tr, d), lambda i: (i, 0)),
        compiler_params=_params("parallel"),
        name="rmsnorm",
    )(x, g.reshape(1, d))


def _inproj_kernel(h_ref, w_ref, o_ref):
    o_ref[...] = _dot(h_ref[...], w_ref[...].astype(BF16))


def _inproj(h, w, layer):
    n, d = h.shape
    cols = w.shape[2]
    bm, bn = _tile(n, 1024), _tile(cols, 512)
    return pl.pallas_call(
        _inproj_kernel,
        out_shape=jax.ShapeDtypeStruct((n, cols), F32),
        grid=(n // bm, cols // bn),
        in_specs=[pl.BlockSpec((bm, d), lambda i, j: (i, 0)),
                  pl.BlockSpec((None, d, bn), lambda i, j: (layer, 0, j))],
        out_specs=pl.BlockSpec((bm, bn), lambda i, j: (i, j)),
        compiler_params=_params("parallel", "arbitrary"),
        name="inproj",
    )(h, w)


def _split3(x):
    hi = x.astype(BF16)
    r1 = x - hi.astype(F32)
    mid = r1.astype(BF16)
    lo = (r1 - mid.astype(F32)).astype(BF16)
    return hi, mid, lo


def _hgrn_kernel(lbl_ref, g_ref, q_ref, f_ref, i_ref, og_ref, o_ref,
                 st_ref, kp_ref, bp_ref, vp_ref, oc_ref, ac_ref, lm_ref, *, layer, n_chunks, n_heads):
    C, PAD = HGRN_CHUNK, HGRN_PAD
    LEVEL_BLOCKS = HGRN_LEVEL_BLOCKS
    dk = g_ref.shape[1]

    @pl.when(pl.program_id(2) == 0)
    def _():
        st_ref[...] = jnp.zeros_like(st_ref)

    zpad = jnp.zeros((PAD, dk), F32)
    for hh in range(n_heads):
        kp_ref[hh, 0:PAD, :] = zpad
        bp_ref[hh, 0:PAD, :] = zpad
        vp_ref[hh, 0:PAD, :] = zpad

    t_i = lax.broadcasted_iota(jnp.int32, (C, C), 0)
    s_i = lax.broadcasted_iota(jnp.int32, (C, C), 1)
    for lvl, blk in enumerate(LEVEL_BLOCKS):
        same = (t_i // blk) == (s_i // blk)
        take = same & ((t_i % blk) >= blk // 2) & ((s_i % blk) < blk // 2)
        lm_ref[lvl] = jnp.where(take, 1.0, 0.0)
    sub8 = lax.broadcasted_iota(jnp.int32, (8, dk), 0)

    lbl = lbl_ref[...]
    e = jnp.exp(lbl - jnp.max(lbl, axis=0, keepdims=True))
    p = e / jnp.sum(e, axis=0, keepdims=True)
    lb_all = jnp.sum(p[0:layer + 1], axis=0, keepdims=True) - p[0:1]
    gain = g_ref[...]

    tri = jnp.where(t_i >= s_i, 1.0, 0.0).astype(BF16)
    tri3 = jnp.concatenate([tri, tri, tri], axis=1)


    def gates(hh, r0):
        rows, cols = pl.ds(r0, C), slice(hh * dk, (hh + 1) * dk)
        lb = lb_all[:, cols]
        z = f_ref[rows, cols]
        gate = (1.0 - lb) * _sigmoid(z)
        f = lb + gate
        lf = jnp.log(jnp.maximum(f, MIN_FORGET))
        k = (1.0 - lb) - gate
        hi, mid, lo = _split3(lf)
        b2 = _dot(tri3, jnp.concatenate([hi, mid, lo], axis=0)) * LOG2E
        kp_ref[hh, PAD:PAD + C, :] = k
        bp_ref[hh, PAD:PAD + C, :] = b2
        vp_ref[hh, PAD:PAD + C, :] = i_ref[rows, cols]

    def scores(hh, r0):
        rows, cols = pl.ds(r0, C), slice(hh * dk, (hh + 1) * dk)
        q = q_ref[rows, cols]
        k, b2 = kp_ref[hh, PAD:PAD + C, :], bp_ref[hh, PAD:PAD + C, :]
        oc_ref[hh] = _dot_nt((q * jnp.exp2(b2)).astype(BF16), st_ref[hh].astype(BF16))

        a = jnp.zeros((C, C), F32)
        for lvl, blk in enumerate(LEVEL_BLOCKS):
            hb_ = blk // 2
            if hb_ >= 8:
                xs_, es_ = [], []
                for b0 in range(0, C, blk):
                    m = bp_ref[hh, PAD + b0 + hb_ - 1:PAD + b0 + hb_, :]
                    xs_ += [k[b0:b0 + hb_], q[b0 + hb_:b0 + blk]]
                    es_ += [m - b2[b0:b0 + hb_], b2[b0 + hb_:b0 + blk] - m]
                x = jnp.concatenate(xs_, axis=0)
                e = jnp.minimum(jnp.concatenate(es_, axis=0), 0.0)
            else:
                up8 = (sub8 & (blk - 1)) >= hb_
                xs_, es_ = [], []
                for g0 in range(0, C, 8):
                    rows_m = [bp_ref[hh, PAD + b0 + hb_ - 1:PAD + b0 + hb_, :] for b0 in range(g0, g0 + 8, blk)]
                    mg = jnp.broadcast_to(rows_m[-1], (8, dk))
                    for j in range(len(rows_m) - 2, -1, -1):
                        mg = jnp.where(sub8 < (j + 1) * blk, rows_m[j], mg)
                    xs_.append(jnp.where(up8, q[g0:g0 + 8], k[g0:g0 + 8]))
                    es_.append(-jnp.abs(b2[g0:g0 + 8] - mg))
                x = jnp.concatenate(xs_, axis=0)
                e = jnp.concatenate(es_, axis=0)
            y = (x * jnp.exp2(e)).astype(BF16)
            a = a + _dot_nt(y, y) * lm_ref[lvl]
        ac_ref[hh] = a.astype(BF16)

    def outputs(hh, r0):
        rows, cols = pl.ds(r0, C), slice(hh * dk, (hh + 1) * dk)
        q, v = q_ref[rows, cols], i_ref[rows, cols]
        k, b2 = kp_ref[hh, PAD:PAD + C, :], bp_ref[hh, PAD:PAD + C, :]
        o = oc_ref[hh] + _dot(ac_ref[hh], v.astype(BF16))
        k1, b1, v1 = kp_ref[hh, PAD - 1:PAD - 1 + C, :], bp_ref[hh, PAD - 1:PAD - 1 + C, :], vp_ref[hh, PAD - 1:PAD - 1 + C, :]
        a1 = jnp.sum(q * k1 * jnp.exp2(jnp.minimum(b2 - b1, 0.0)), axis=-1, keepdims=True)
        a0 = jnp.sum(q * k, axis=-1, keepdims=True)
        odd8 = (sub8 & 1) == 1
        a1 = jnp.concatenate([jnp.where(odd8, a1[g0:g0 + 8], 0.0) for g0 in range(0, C, 8)], axis=0)
        oc_ref[hh] = o + a1 * v1 + a0 * v

    def finish(hh, r0):
        rows, cols = pl.ds(r0, C), slice(hh * dk, (hh + 1) * dk)
        k, b2, v = kp_ref[hh, PAD:PAD + C, :], bp_ref[hh, PAD:PAD + C, :], i_ref[rows, cols]
        b_last = b2[C - 1:C, :]
        khat = k * jnp.exp2(b_last - b2)
        st_ref[hh] = st_ref[hh] * jnp.exp2(b_last) + _dot(v.T.astype(BF16), khat.astype(BF16))
        og = og_ref[rows, cols]
        y = _rms(oc_ref[hh], gain) * (og * _sigmoid(og))
        o_ref[rows, cols] = y.astype(o_ref.dtype)

    def chunk(c, carry):
        r0 = pl.multiple_of(c * C, C)
        for phase in (gates, scores, outputs, finish):
            for hh in range(n_heads):
                phase(hh, r0)
        return carry

    lax.fori_loop(0, n_chunks, chunk, 0)


def _hgrn(proj, lb_logits, gain, *, layer, batch, seq, width):
    n = proj.shape[0]
    dk = gain.shape[-1]
    heads = width // dk
    hb = _tile(heads, 16)
    hg = heads // hb
    t = _tile(seq, 512)
    ns = seq // t
    n_layers = lb_logits.shape[0]

    def col(seg):
        return pl.BlockSpec((t, hb * dk), lambda b, h, s, seg=seg: (b * ns + s, seg * hg + h))

    kern = functools.partial(_hgrn_kernel, layer=layer, n_chunks=t // HGRN_CHUNK, n_heads=hb)
    buf = pltpu.VMEM((hb, HGRN_PAD + HGRN_CHUNK, dk), F32)
    cbuf = pltpu.VMEM((hb, HGRN_CHUNK, dk), F32)
    masks = pltpu.VMEM((len(HGRN_LEVEL_BLOCKS), HGRN_CHUNK, HGRN_CHUNK), F32)
    return pl.pallas_call(
        kern,
        out_shape=jax.ShapeDtypeStruct((n, width), BF16),
        grid=(batch, hg, ns),
        in_specs=[pl.BlockSpec((n_layers, hb * dk), lambda b, h, s: (0, h)),
                  pl.BlockSpec((1, dk), lambda b, h, s: (0, 0)),
                  col(0), col(1), col(2), col(3)],
        out_specs=pl.BlockSpec((t, hb * dk), lambda b, h, s: (b * ns + s, h)),
        scratch_shapes=[pltpu.VMEM((hb, dk, dk), F32), buf, buf, buf, cbuf,
                        pltpu.VMEM((hb, HGRN_CHUNK, HGRN_CHUNK), BF16), masks],
        compiler_params=_params("parallel", "parallel", "arbitrary"),
        name="hgrn2",
    )(lb_logits, gain.reshape(1, dk), proj, proj, proj, proj)


def _pool_kernel(u_ref, halo_ref, w_ref, sc_ref, o_ref, *, windows):
    g = pl.program_id(2)
    s = pl.program_id(1)
    t = u_ref.shape[0]
    u = u_ref[...]
    win = jnp.int32(windows[0])
    for gi in range(1, len(windows)):
        win = jnp.where(g == gi, jnp.int32(windows[gi]), win)
    r_i = lax.broadcasted_iota(jnp.int32, (t, POOL_HALO + t), 0)
    c_i = lax.broadcasted_iota(jnp.int32, (t, POOL_HALO + t), 1) - POOL_HALO
    lag = r_i - c_i
    inside = (lag >= 0) & (lag < win) & (c_i + s * t >= 0)
    band = jnp.where(inside, 1.0, 0.0).astype(BF16)
    ext = jnp.concatenate([halo_ref[...].astype(BF16), u.astype(BF16)], axis=0)
    wsum = _dot(band, ext)
    pos = (lax.broadcasted_iota(jnp.int32, (t, 1), 0) + s * t + 1).astype(F32)
    pooled = wsum / jnp.minimum(pos, win.astype(F32)) - u
    mixed = _dot(pooled.astype(BF16), w_ref[...].astype(BF16))
    o_ref[...] = (mixed * sc_ref[...]).astype(o_ref.dtype)


def _pool(proj, w_pool, pool_scale, *, layer, batch, seq, col0):
    n = proj.shape[0]
    _, groups, cg, _ = w_pool.shape
    t = _tile(seq, 512)
    ns = seq // t
    hb = t // POOL_HALO
    assert col0 % cg == 0 and groups == len(POOL_WINDOWS)
    cb = col0 // cg
    kern = functools.partial(_pool_kernel, windows=POOL_WINDOWS)
    return pl.pallas_call(
        kern,
        out_shape=jax.ShapeDtypeStruct((n, groups * cg), BF16),
        grid=(batch, ns, groups),
        in_specs=[pl.BlockSpec((t, cg), lambda b, s, g: (b * ns + s, cb + g)),
                  pl.BlockSpec((POOL_HALO, cg),
                               lambda b, s, g: (jnp.maximum((b * ns + s) * hb - 1, 0), cb + g)),
                  pl.BlockSpec((None, None, cg, cg), lambda b, s, g: (layer, g, 0, 0)),
                  pl.BlockSpec((1, cg), lambda b, s, g: (0, g))],
        out_specs=pl.BlockSpec((t, cg), lambda b, s, g: (b * ns + s, g)),
        compiler_params=_params("parallel", "parallel", "arbitrary"),
        name="pool_mixer",
    )(proj, proj, w_pool, pool_scale[layer].reshape(1, groups * cg))


def _merge_kernel(oa_ref, ob_ref, wa_ref, wb_ref, ga_ref, gb_ref, o_ref):
    a = _dot(oa_ref[...], wa_ref[...].astype(BF16))
    b = _dot(ob_ref[...], wb_ref[...].astype(BF16))
    o_ref[...] = (_sigmoid(ga_ref[...]) * a + _sigmoid(gb_ref[...]) * b).astype(o_ref.dtype)


def _merge(o_a, o_b, w_a, w_b, proj, *, layer, col_ga, col_gb):
    n, wa = o_a.shape
    wb = o_b.shape[1]
    d = w_a.shape[2]
    bm, bn = _tile(n, 1024), _tile(d, 512)
    assert col_ga % bn == 0 and col_gb % bn == 0
    ca, cb = col_ga // bn, col_gb // bn
    return pl.pallas_call(
        _merge_kernel,
        out_shape=jax.ShapeDtypeStruct((n, d), BF16),
        grid=(n // bm, d // bn),
        in_specs=[pl.BlockSpec((bm, wa), lambda i, j: (i, 0)),
                  pl.BlockSpec((bm, wb), lambda i, j: (i, 0)),
                  pl.BlockSpec((None, wa, bn), lambda i, j: (layer, 0, j)),
                  pl.BlockSpec((None, wb, bn), lambda i, j: (layer, 0, j)),
                  pl.BlockSpec((bm, bn), lambda i, j: (i, ca + j)),
                  pl.BlockSpec((bm, bn), lambda i, j: (i, cb + j))],
        out_specs=pl.BlockSpec((bm, bn), lambda i, j: (i, j)),
        compiler_params=_params("parallel", "arbitrary"),
        name="gated_merge",
    )(o_a, o_b, w_a, w_b, proj, proj)


def _outproj_kernel(m_ref, w_ref, x_ref, o_ref):
    o_ref[...] = x_ref[...] + _dot(m_ref[...], w_ref[...].astype(BF16))


def _outproj(merged, w, x, layer):
    n, d = merged.shape
    dn = w.shape[2]
    bm, bn = _tile(n, 1024), _tile(dn, 512)
    return pl.pallas_call(
        _outproj_kernel,
        out_shape=jax.ShapeDtypeStruct((n, dn), F32),
        grid=(n // bm, dn // bn),
        in_specs=[pl.BlockSpec((bm, d), lambda i, j: (i, 0)),
                  pl.BlockSpec((None, d, bn), lambda i, j: (layer, 0, j)),
                  pl.BlockSpec((bm, bn), lambda i, j: (i, j))],
        out_specs=pl.BlockSpec((bm, bn), lambda i, j: (i, j)),
        compiler_params=_params("parallel", "arbitrary"),
        name="outproj_residual",
    )(merged, w, x)


def _router_kernel(x_ref, g_ref, wr_ref, hp_ref, route_ref, cnt_ref, run_ref, *, n_groups, n_experts):
    tr, d = x_ref.shape

    @pl.when(pl.program_id(0) == 0)
    def _():
        run_ref[...] = jnp.zeros_like(run_ref)

    h = _rms(x_ref[...], g_ref[...])
    hb = h.astype(BF16)
    hp_ref[...] = h

    h_lo = (h - hb.astype(F32)).astype(BF16)
    wr = wr_ref[...]
    w_hi = wr.astype(BF16)
    w_lo = (wr - w_hi.astype(F32)).astype(BF16)
    logits = _dot(hb, w_hi) + _dot(hb, w_lo) + _dot(h_lo, w_hi)

    lane = lax.broadcasted_iota(jnp.int32, (tr, LANES), 1).astype(F32)
    ninf = jnp.float32(-jnp.inf)
    big = jnp.float32(1e9)
    is_g = lane < n_groups
    gl = jnp.where(is_g, logits, ninf)
    gmax = jnp.max(gl, axis=-1, keepdims=True)
    gidx = jnp.min(jnp.where(gl == gmax, lane, big), axis=-1, keepdims=True)
    g_w = 1.0 / jnp.sum(jnp.where(is_g, jnp.exp(gl - gmax), 0.0), axis=-1, keepdims=True)

    e_lo = n_groups + gidx * n_experts
    el = jnp.where(lane >= e_lo, jnp.where(lane < e_lo + n_experts, logits, ninf), ninf)
    v1 = jnp.max(el, axis=-1, keepdims=True)
    i1 = jnp.min(jnp.where(el == v1, lane, big), axis=-1, keepdims=True)
    el2 = jnp.where(lane == i1, ninf, el)
    v2 = jnp.max(el2, axis=-1, keepdims=True)
    i2 = jnp.min(jnp.where(el2 == v2, lane, big), axis=-1, keepdims=True)
    tt = jnp.exp(v2 - v1)
    w1 = g_w / (1.0 + tt)
    w2 = g_w * tt / (1.0 + tt)
    e1 = i1 - n_groups
    e2 = i2 - n_groups

    oh1 = lane == e1
    oh2 = lane == e2
    cnt = jnp.where(oh1, 1.0, 0.0) + jnp.where(oh2, 1.0, 0.0)
    r_i = lax.broadcasted_iota(jnp.int32, (tr, tr), 0)
    c_i = lax.broadcasted_iota(jnp.int32, (tr, tr), 1)
    strict = jnp.where(r_i > c_i, 1.0, 0.0).astype(BF16)
    before = _dot(strict, cnt.astype(BF16)) + run_ref[0:1, :]
    rank1 = jnp.sum(jnp.where(oh1, before, 0.0), axis=-1, keepdims=True)
    rank2 = jnp.sum(jnp.where(oh2, before, 0.0), axis=-1, keepdims=True)
    run = run_ref[0:1, :] + jnp.sum(cnt, axis=0, keepdims=True)
    run_ref[...] = jnp.broadcast_to(run, run_ref.shape)
    cnt_ref[...] = jnp.broadcast_to(run, cnt_ref.shape)

    out = jnp.zeros((tr, LANES), F32)
    for idx, val in enumerate((e1, e2, w1, w2, rank1, rank2)):
        out = jnp.where(lane == idx, val, out)
    route_ref[...] = out[:, :ROUTE_COLS]


def _router(x, g, w_router, *, n_groups, n_experts):
    n, d = x.shape
    tr = _tile(n, 256)
    kern = functools.partial(_router_kernel, n_groups=n_groups, n_experts=n_experts)
    return pl.pallas_call(
        kern,
        out_shape=(jax.ShapeDtypeStruct((n, d), F32),
                   jax.ShapeDtypeStruct((n, ROUTE_COLS), F32),
                   jax.ShapeDtypeStruct((8, LANES), F32)),
        grid=(n // tr,),
        in_specs=[pl.BlockSpec((tr, d), lambda i: (i, 0)),
                  pl.BlockSpec((1, d), lambda i: (0, 0)),
                  pl.BlockSpec((d, LANES), lambda i: (0, 0))],
        out_specs=(pl.BlockSpec((tr, d), lambda i: (i, 0)),
                   pl.BlockSpec((tr, ROUTE_COLS), lambda i: (i, 0)),
                   pl.BlockSpec((8, LANES), lambda i: (0, 0))),
        scratch_shapes=[pltpu.VMEM((8, LANES), F32)],
        compiler_params=_params("arbitrary"),
        name="norm_router",
    )(x, g.reshape(1, d), w_router)


def _row_copy(src_ref, src_row, dst_ref, dst_row, sem):
    return pltpu.make_async_copy(src_ref.at[pl.ds(src_row, 1)], dst_ref.at[pl.ds(dst_row, 1)], sem)


def _start_rows(copies, row0, n_rows):
    unroll = DMA_ISSUE_UNROLL

    def issue(g, carry):
        for u in range(unroll):
            for c in copies(row0 + g * unroll + u):
                c.start()
        return carry

    lax.fori_loop(0, n_rows // unroll, issue, 0)


def _wait_rows(copies, row0, n_rows):
    unroll = DMA_ISSUE_UNROLL
    first = copies(row0)

    def drain(g, carry):
        for _ in range(unroll):
            for c in first:
                c.wait()
        return carry

    lax.fori_loop(0, n_rows // unroll, drain, 0)


def _start_then_wait_rows(copies, n_rows):
    _start_rows(copies, 0, n_rows)
    _wait_rows(copies, 0, n_rows)


def _dest_row(code_ref, offs_ref, t):
    code = code_ref[t]
    return offs_ref[code >> ROUTE_CODE_SHIFT] + (code & ((1 << ROUTE_CODE_SHIFT) - 1))


def _dispatch_kernel(c1_ref, c2_ref, offs_ref, zb_ref, zf_ref, h_ref, xs_ref, zero_ref, sem, *, n_exp):
    tr = h_ref.shape[0]
    sub = zero_ref.shape[0]
    base = pl.program_id(0) * tr

    def zero_copy(e):
        dst = xs_ref.at[pl.ds(pl.multiple_of(zb_ref[e] * sub, sub), sub)]
        return pltpu.make_async_copy(zero_ref, dst, sem.at[2])

    @pl.when(pl.program_id(0) == 0)
    def _():
        zero_ref[...] = jnp.zeros_like(zero_ref)

        def z_start(e, carry):
            @pl.when(zf_ref[e] != 0)
            def _():
                zero_copy(e).start()
            return carry

        def z_wait(e, carry):
            @pl.when(zf_ref[e] != 0)
            def _():
                zero_copy(e).wait()
            return carry

        lax.fori_loop(0, n_exp, z_start, 0)
        lax.fori_loop(0, n_exp, z_wait, 0)

    def copies(r):
        t = base + r
        return (_row_copy(h_ref, r, xs_ref, _dest_row(c1_ref, offs_ref, t), sem.at[0]),
                _row_copy(h_ref, r, xs_ref, _dest_row(c2_ref, offs_ref, t), sem.at[1]))

    _start_then_wait_rows(copies, tr)


def _dispatch(hp, code1, code2, offs, zero_block, zero_flag, *, rows, sub):
    n, dw = hp.shape
    tr = _tile(n, 256)
    grid_spec = pltpu.PrefetchScalarGridSpec(
        num_scalar_prefetch=5,
        grid=(n // tr,),
        in_specs=[pl.BlockSpec((tr, dw), lambda i, *_: (i, 0))],
        out_specs=pl.BlockSpec(memory_space=pl.ANY),
        scratch_shapes=[pltpu.VMEM((sub, dw), hp.dtype), pltpu.SemaphoreType.DMA((3,))],
    )
    return pl.pallas_call(
        functools.partial(_dispatch_kernel, n_exp=zero_block.shape[0]),
        out_shape=jax.ShapeDtypeStruct((rows, dw), hp.dtype),
        grid_spec=grid_spec,
        compiler_params=_params("arbitrary"),
        name="dispatch",
    )(code1, code2, offs, zero_block, zero_flag, hp)


def _expert_kernel(se_ref, sb_ref, sn_ref, ns_ref, xs_ref, wg_ref, wu_ref, wd_ref, o_ref,
                   hid_ref, *, n_up, sub):
    del se_ref, sb_ref
    s, p = pl.program_id(0), pl.program_id(1)
    valid = s < ns_ref[0]
    hsub = sub // 2
    n_half = sn_ref[s]
    n_full = n_half // 2
    odd = n_half % 2 == 1
    fc = wg_ref.shape[1]

    def rows_at(q_half, size):
        return pl.ds(pl.multiple_of(q_half * hsub, hsub), size)

    @pl.when(jnp.logical_and(valid, p < n_up))
    def _():
        wg = wg_ref[...].astype(BF16)
        wu = wu_ref[...].astype(BF16)

        def up(rows):
            x = xs_ref[rows, :].astype(BF16)
            a = _dot(x, wg)
            b = _dot(x, wu)
            hid_ref[p, rows, :] = (a * _sigmoid(a) * b).astype(BF16)

        def body(q, carry):
            up(rows_at(2 * q, sub))
            return carry

        lax.fori_loop(0, n_full, body, 0)

        @pl.when(odd)
        def _():
            up(rows_at(2 * n_full, hsub))

    @pl.when(jnp.logical_and(valid, p >= n_up))
    def _():
        wd = wd_ref[...].astype(BF16)

        def down(rows):
            y = _dot(hid_ref[0, rows, :], wd[0:fc])
            for j in range(1, n_up):
                y += _dot(hid_ref[j, rows, :], wd[j * fc:(j + 1) * fc])
            o_ref[rows, :] = y

        def body(q, carry):
            down(rows_at(2 * q, sub))
            return carry

        def zero_body(q, carry):
            o_ref[rows_at(q, hsub), :] = jnp.zeros((hsub, o_ref.shape[1]), o_ref.dtype)
            return carry

        lax.fori_loop(0, n_full, body, 0)

        @pl.when(odd)
        def _():
            down(rows_at(2 * n_full, hsub))

        lax.fori_loop(n_half, xs_ref.shape[0] // hsub, zero_body, 0)


def _experts(xs, w_gate, w_up, w_down, sup_expert, sup_block, sup_nsub, n_super, *, sub, sup):
    rows = xs.shape[0]
    _, d, f = w_gate.shape
    fc = _tile(f, 256)
    n_up = f // fc
    dc = _tile(d, min(1024, d // 2))
    n_down = d // dc
    n_phase = n_up + n_down

    def phase(s, p, ns):
        return jnp.where(s < ns[0], p, n_phase - 1)

    def x_map(s, p, se, sb, sn, ns):
        return (sb[s + (phase(s, p, ns) >= n_up).astype(jnp.int32)], 0)

    def up_map(first_down_phase):
        def index_map(s, p, se, sb, sn, ns):
            ph = phase(s, p, ns)
            ahead = ph >= n_up + first_down_phase
            return (se[s + ahead.astype(jnp.int32)], 0, jnp.where(ahead, 0, jnp.minimum(ph, n_up - 1)))
        return index_map

    def down_c(s, p, ns):
        return jnp.maximum(phase(s, p, ns) - n_up, 0)

    def down_map(s, p, se, sb, sn, ns):
        early = jnp.logical_and(phase(s, p, ns) < n_up - 1, s > 0)
        expert = se[jnp.where(early, s - 1, s)]
        return (expert, 0, jnp.where(early, n_down - 1, down_c(s, p, ns)))

    grid_spec = pltpu.PrefetchScalarGridSpec(
        num_scalar_prefetch=4,
        grid=(rows // sup, n_phase),
        in_specs=[pl.BlockSpec((sup, d), x_map),
                  pl.BlockSpec((None, d, fc), up_map(0)),
                  pl.BlockSpec((None, d, fc), up_map(1)),
                  pl.BlockSpec((None, f, dc), down_map)],
        out_specs=pl.BlockSpec((sup, dc), lambda s, p, se, sb, sn, ns: (sb[s], down_c(s, p, ns))),
        scratch_shapes=[pltpu.VMEM((n_up, sup, fc), BF16)],
    )
    return pl.pallas_call(
        functools.partial(_expert_kernel, n_up=n_up, sub=sub),
        out_shape=jax.ShapeDtypeStruct((rows, d), F32),
        grid_spec=grid_spec,
        compiler_params=_params("arbitrary", "arbitrary"),
        name="expert_mlp",
    )(sup_expert, sup_block, sup_nsub, n_super, xs, w_gate, w_up, w_down)


def _combine_kernel(c1_ref, c2_ref, offs_ref, x_ref, route_ref, g_ref, ys_ref, *rest, final):
    if final:
        y_ref, buf1, buf2, sem = rest
    else:
        x2_ref, hn_ref, buf1, buf2, sem = rest
    tr = x_ref.shape[0]
    base = pl.program_id(0) * tr

    out_ref = y_ref if final else hn_ref
    n_parts = sem.shape[0]
    part = tr // n_parts

    def copies_of(p):
        def copies(r):
            t = base + r
            return (_row_copy(ys_ref, _dest_row(c1_ref, offs_ref, t), buf1, r, sem.at[p, 0]),
                    _row_copy(ys_ref, _dest_row(c2_ref, offs_ref, t), buf2, r, sem.at[p, 1]))
        return copies

    for p in range(n_parts):
        _start_rows(copies_of(p), p * part, part)

    for p in range(n_parts):
        _wait_rows(copies_of(p), p * part, part)
        rows = slice(p * part, (p + 1) * part)
        route = route_ref[rows, :]
        w1, w2 = route[:, 2:3], route[:, 3:4]
        x2 = x_ref[rows, :] + w1 * buf1[rows, :] + w2 * buf2[rows, :]
        if not final:
            x2_ref[rows, :] = x2
        out_ref[rows, :] = _rms(x2, g_ref[...]).astype(out_ref.dtype)


def _combine(x, route, g, ys, code1, code2, offs, *, final):
    n, d = x.shape
    tr = _tile(n, 256)
    row = lambda i, *_: (i, 0)
    if final:
        out_shape = jax.ShapeDtypeStruct((n, d), F32)
        out_specs = pl.BlockSpec((tr, d), row)
    else:
        out_shape = (jax.ShapeDtypeStruct((n, d), F32), jax.ShapeDtypeStruct((n, d), BF16))
        out_specs = (pl.BlockSpec((tr, d), row), pl.BlockSpec((tr, d), row))
    grid_spec = pltpu.PrefetchScalarGridSpec(
        num_scalar_prefetch=3,
        grid=(n // tr,),
        in_specs=[pl.BlockSpec((tr, d), row),
                  pl.BlockSpec((tr, ROUTE_COLS), row),
                  pl.BlockSpec((1, d), lambda i, *_: (0, 0)),
                  pl.BlockSpec(memory_space=pl.ANY)],
        out_specs=out_specs,
        scratch_shapes=[pltpu.VMEM((tr, d), F32), pltpu.VMEM((tr, d), F32),
                        pltpu.SemaphoreType.DMA((COMBINE_PARTS, 2))],
    )
    return pl.pallas_call(
        functools.partial(_combine_kernel, final=final),
        out_shape=out_shape,
        grid_spec=grid_spec,
        compiler_params=_params("arbitrary"),
        name="combine_final" if final else "combine",
    )(code1, code2, offs, x, route, g.reshape(1, d), ys)


def _plan(route, counts, *, ge, sub, sup, max_super):
    i32 = jnp.int32
    e1, e2 = route[:, 0].astype(i32), route[:, 1].astype(i32)
    rank1, rank2 = route[:, 4].astype(i32), route[:, 5].astype(i32)
    cnt = counts[0, :ge].astype(i32)
    n_sub = (cnt + sub - 1) // sub
    n_sup = (cnt + sup - 1) // sup
    ends = jnp.cumsum(n_sup * sup)
    offs = ends - n_sup * sup
    code1 = e1 * (1 << ROUTE_CODE_SHIFT) + rank1
    code2 = e2 * (1 << ROUTE_CODE_SHIFT) + rank2
    sup_ends = jnp.cumsum(n_sup)
    n_super = sup_ends[-1]
    s = jnp.arange(max_super + 1, dtype=i32)
    se = jnp.minimum(jnp.sum((sup_ends[None, :] <= s[:, None]).astype(i32), axis=1), ge - 1)
    k = s - (sup_ends[se] - n_sup[se])
    sb = offs[se] // sup + k
    sn = jnp.clip(n_sub[se] - k * (sup // sub), 0, sup // sub)
    last = jnp.maximum(n_super - 1, 0)
    valid = s < n_super
    sup_expert = jnp.where(valid, se, se[last]).astype(i32)
    sup_block = jnp.where(valid, sb, sb[last]).astype(i32)
    sup_nsub = jnp.where(valid, sn, 0).astype(i32)
    zero_block = (offs // sub + cnt // sub).astype(i32)
    zero_flag = (cnt % sub != 0).astype(i32)
    return (code1, code2, offs.astype(i32), sup_expert, sup_block, sup_nsub,
            n_super.reshape(1).astype(i32), zero_block, zero_flag)


def kernel(x, norm1_g, w_in, lb_logits, hgrn_norm_g, w_pool, pool_scale, w_branch_a, w_branch_b,
           w_out, norm2_g, w_router_group, w_router_expert, w_gate, w_up, w_down, final_norm_g):
    batch, seq, d = x.shape
    n = batch * seq
    depth = w_in.shape[0]
    width = lb_logits.shape[1]
    pool_width = pool_scale.shape[1]
    n_groups = w_router_group.shape[-1]
    n_experts = w_gate.shape[2]
    ge = n_groups * n_experts
    d_expert = w_gate.shape[-1]
    assert n_groups + ge <= LANES and n < (1 << ROUTE_CODE_SHIFT)
    col_pool = 4 * width
    col_ga = col_pool + pool_width
    col_gb = col_ga + d

    sub = MOE_SUB_ROWS if n >= 4096 else 32
    sup = MOE_HALVES_PER_SUPER * (sub // 2)
    max_super = (TOP_K * n + ge * (sup - 1)) // sup
    rows = max_super * sup

    wg_all = w_gate.reshape(depth * ge, d, d_expert)
    wu_all = w_up.reshape(depth * ge, d, d_expert)
    wd_all = w_down.reshape(depth * ge, d_expert, d)

    xf = x.reshape(n, d)
    h = _rmsnorm(xf, norm1_g[0], BF16)
    for l in range(depth):
        proj = _inproj(h, w_in, l)
        o_a = _hgrn(proj, lb_logits, hgrn_norm_g[l], layer=l, batch=batch, seq=seq, width=width)
        o_b = _pool(proj, w_pool, pool_scale, layer=l, batch=batch, seq=seq, col0=col_pool)
        merged = _merge(o_a, o_b, w_branch_a, w_branch_b, proj, layer=l, col_ga=col_ga, col_gb=col_gb)
        x1 = _outproj(merged, w_out, xf, l)

        w_router = jnp.concatenate([w_router_group[l], w_router_expert[l]], axis=1)
        w_router = jnp.pad(w_router, ((0, 0), (0, LANES - w_router.shape[1])))
        hp, route, counts = _router(x1, norm2_g[l], w_router, n_groups=n_groups, n_experts=n_experts)
        (code1, code2, offs, sup_expert, sup_block, sup_nsub, n_super, zero_block, zero_flag) = _plan(
            route, counts, ge=ge, sub=sub // 2, sup=sup, max_super=max_super)
        xs = _dispatch(hp, code1, code2, offs, zero_block, zero_flag, rows=rows, sub=sub // 2)
        ys = _experts(xs, wg_all, wu_all, wd_all, sup_expert + l * ge, sup_block, sup_nsub, n_super,
                      sub=sub, sup=sup)
        if l + 1 < depth:
            xf, h = _combine(x1, route, norm1_g[l + 1], ys, code1, code2, offs, final=False)
        else:
            out = _combine(x1, route, final_norm_g, ys, code1, code2, offs, final=True)
    return out.reshape(batch, seq, d)
```

```python
import functools

import jax
import jax.numpy as jnp
from jax import lax
from jax.experimental import pallas as pl
from jax.experimental.pallas import tpu as pltpu

RMS_EPS = 1e-6
MIN_FORGET = 1e-30
LOG2E = 1.4426950408889634
POOL_WINDOWS = (2, 4, 8, 16)
POOL_HALO = 16
HGRN_CHUNK = 64
HGRN_PAD = 8
HGRN_LEVEL_BLOCKS = (64, 32, 16, 8, 4)
TOP_K = 2
MOE_SUB_ROWS = 256
MOE_HALVES_PER_SUPER = 5
LANES = 128
ROUTE_COLS = 8
ROUTE_CODE_SHIFT = 20
DMA_ISSUE_UNROLL = 8
VMEM_LIMIT_BYTES = 56 * 1024 * 1024

F32 = jnp.float32
BF16 = jnp.bfloat16


def _params(*sem):
    return pltpu.CompilerParams(dimension_semantics=sem, vmem_limit_bytes=VMEM_LIMIT_BYTES)


def _tile(n, pref):
    t = min(n, pref)
    assert n % t == 0, (n, pref)
    return t


def _sigmoid(z):
    return 1.0 / (1.0 + jnp.exp(-z))


def _dot(a, b):
    return jnp.dot(a, b, preferred_element_type=F32)


def _dot_nt(a, b):
    return lax.dot_general(a, b, (((1,), (1,)), ((), ())), preferred_element_type=F32)


def _rms(x, g):
    return x * lax.rsqrt(jnp.mean(x * x, axis=-1, keepdims=True) + RMS_EPS) * g


def _rmsnorm_kernel(x_ref, g_ref, o_ref):
    o_ref[...] = _rms(x_ref[...], g_ref[...]).astype(o_ref.dtype)


def _rmsnorm(x, g, out_dtype):
    n, d = x.shape
    tr = _tile(n, 256)
    return pl.pallas_call(
        _rmsnorm_kernel,
        out_shape=jax.ShapeDtypeStruct((n, d), out_dtype),
        grid=(n // tr,),
        in_specs=[pl.BlockSpec((tr, d), lambda i: (i, 0)),
                  pl.BlockSpec((1, d), lambda i: (0, 0))],
        out_specs=pl.BlockSpec((tr, d), lambda i: (i, 0)),
        compiler_params=_params("parallel"),
        name="rmsnorm",
    )(x, g.reshape(1, d))


def _inproj_kernel(h_ref, w_ref, o_ref):
    o_ref[...] = _dot(h_ref[...], w_ref[...].astype(BF16))


def _inproj(h, w, layer):
    n, d = h.shape
    cols = w.shape[2]
    bm, bn = _tile(n, 1024), _tile(cols, 512)
    return pl.pallas_call(
        _inproj_kernel,
        out_shape=jax.ShapeDtypeStruct((n, cols), F32),
        grid=(n // bm, cols // bn),
        in_specs=[pl.BlockSpec((bm, d), lambda i, j: (i, 0)),
                  pl.BlockSpec((None, d, bn), lambda i, j: (layer, 0, j))],
        out_specs=pl.BlockSpec((bm, bn), lambda i, j: (i, j)),
        compiler_params=_params("parallel", "arbitrary"),
        name="inproj",
    )(h, w)


def _split3(x):
    hi = x.astype(BF16)
    r1 = x - hi.astype(F32)
    mid = r1.astype(BF16)
    lo = (r1 - mid.astype(F32)).astype(BF16)
    return hi, mid, lo


def _hgrn_kernel(lbl_ref, g_ref, q_ref, f_ref, i_ref, og_ref, o_ref,
                 st_ref, kp_ref, bp_ref, vp_ref, oc_ref, ac_ref, lm_ref, *, layer, n_chunks, n_heads):
    C, PAD = HGRN_CHUNK, HGRN_PAD
    LEVEL_BLOCKS = HGRN_LEVEL_BLOCKS
    dk = g_ref.shape[1]

    @pl.when(pl.program_id(2) == 0)
    def _():
        st_ref[...] = jnp.zeros_like(st_ref)

    zpad = jnp.zeros((PAD, dk), F32)
    for hh in range(n_heads):
        kp_ref[hh, 0:PAD, :] = zpad
        bp_ref[hh, 0:PAD, :] = zpad
        vp_ref[hh, 0:PAD, :] = zpad

    t_i = lax.broadcasted_iota(jnp.int32, (C, C), 0)
    s_i = lax.broadcasted_iota(jnp.int32, (C, C), 1)
    for lvl, blk in enumerate(LEVEL_BLOCKS):
        same = (t_i // blk) == (s_i // blk)
        take = same & ((t_i % blk) >= blk // 2) & ((s_i % blk) < blk // 2)
        lm_ref[lvl] = jnp.where(take, 1.0, 0.0)
    sub8 = lax.broadcasted_iota(jnp.int32, (8, dk), 0)

    lbl = lbl_ref[...]
    e = jnp.exp(lbl - jnp.max(lbl, axis=0, keepdims=True))
    p = e / jnp.sum(e, axis=0, keepdims=True)
    lb_all = jnp.sum(p[0:layer + 1], axis=0, keepdims=True) - p[0:1]
    gain = g_ref[...]

    tri = jnp.where(t_i >= s_i, 1.0, 0.0).astype(BF16)
    tri3 = jnp.concatenate([tri, tri, tri], axis=1)


    def gates(hh, r0):
        rows, cols = pl.ds(r0, C), slice(hh * dk, (hh + 1) * dk)
        lb = lb_all[:, cols]
        z = f_ref[rows, cols]
        gate = (1.0 - lb) * _sigmoid(z)
        f = lb + gate
        lf = jnp.log(jnp.maximum(f, MIN_FORGET))
        k = (1.0 - lb) - gate
        hi, mid, lo = _split3(lf)
        b2 = _dot(tri3, jnp.concatenate([hi, mid, lo], axis=0)) * LOG2E
        kp_ref[hh, PAD:PAD + C, :] = k
        bp_ref[hh, PAD:PAD + C, :] = b2
        vp_ref[hh, PAD:PAD + C, :] = i_ref[rows, cols]

    def scores(hh, r0):
        rows, cols = pl.ds(r0, C), slice(hh * dk, (hh + 1) * dk)
        q = q_ref[rows, cols]
        k, b2 = kp_ref[hh, PAD:PAD + C, :], bp_ref[hh, PAD:PAD + C, :]
        oc_ref[hh] = _dot_nt((q * jnp.exp2(b2)).astype(BF16), st_ref[hh].astype(BF16))

        a = jnp.zeros((C, C), F32)
        for lvl, blk in enumerate(LEVEL_BLOCKS):
            hb_ = blk // 2
            if hb_ >= 8:
                xs_, es_ = [], []
                for b0 in range(0, C, blk):
                    m = bp_ref[hh, PAD + b0 + hb_ - 1:PAD + b0 + hb_, :]
                    xs_ += [k[b0:b0 + hb_], q[b0 + hb_:b0 + blk]]
                    es_ += [m - b2[b0:b0 + hb_], b2[b0 + hb_:b0 + blk] - m]
                x = jnp.concatenate(xs_, axis=0)
                e = jnp.minimum(jnp.concatenate(es_, axis=0), 0.0)
            else:
                up8 = (sub8 & (blk - 1)) >= hb_
                xs_, es_ = [], []
                for g0 in range(0, C, 8):
                    rows_m = [bp_ref[hh, PAD + b0 + hb_ - 1:PAD + b0 + hb_, :] for b0 in range(g0, g0 + 8, blk)]
                    mg = jnp.broadcast_to(rows_m[-1], (8, dk))
                    for j in range(len(rows_m) - 2, -1, -1):
                        mg = jnp.where(sub8 < (j + 1) * blk, rows_m[j], mg)
                    xs_.append(jnp.where(up8, q[g0:g0 + 8], k[g0:g0 + 8]))
                    es_.append(-jnp.abs(b2[g0:g0 + 8] - mg))
                x = jnp.concatenate(xs_, axis=0)
                e = jnp.concatenate(es_, axis=0)
            y = (x * jnp.exp2(e)).astype(BF16)
            a = a + _dot_nt(y, y) * lm_ref[lvl]
        ac_ref[hh] = a.astype(BF16)

    def outputs(hh, r0):
        rows, cols = pl.ds(r0, C), slice(hh * dk, (hh + 1) * dk)
        q, v = q_ref[rows, cols], i_ref[rows, cols]
        k, b2 = kp_ref[hh, PAD:PAD + C, :], bp_ref[hh, PAD:PAD + C, :]
        o = oc_ref[hh] + _dot(ac_ref[hh], v.astype(BF16))
        k1, b1, v1 = kp_ref[hh, PAD - 1:PAD - 1 + C, :], bp_ref[hh, PAD - 1:PAD - 1 + C, :], vp_ref[hh, PAD - 1:PAD - 1 + C, :]
        a1 = jnp.sum(q * k1 * jnp.exp2(jnp.minimum(b2 - b1, 0.0)), axis=-1, keepdims=True)
        a0 = jnp.sum(q * k, axis=-1, keepdims=True)
        odd8 = (sub8 & 1) == 1
        a1 = jnp.concatenate([jnp.where(odd8, a1[g0:g0 + 8], 0.0) for g0 in range(0, C, 8)], axis=0)
        oc_ref[hh] = o + a1 * v1 + a0 * v

    def finish(hh, r0):
        rows, cols = pl.ds(r0, C), slice(hh * dk, (hh + 1) * dk)
        k, b2, v = kp_ref[hh, PAD:PAD + C, :], bp_ref[hh, PAD:PAD + C, :], i_ref[rows, cols]
        b_last = b2[C - 1:C, :]
        khat = k * jnp.exp2(b_last - b2)
        st_ref[hh] = st_ref[hh] * jnp.exp2(b_last) + _dot(v.T.astype(BF16), khat.astype(BF16))
        og = og_ref[rows, cols]
        y = _rms(oc_ref[hh], gain) * (og * _sigmoid(og))
        o_ref[rows, cols] = y.astype(o_ref.dtype)

    def chunk(c, carry):
        r0 = pl.multiple_of(c * C, C)
        for phase in (gates, scores, outputs, finish):
            for hh in range(n_heads):
                phase(hh, r0)
        return carry

    lax.fori_loop(0, n_chunks, chunk, 0)


def _hgrn(proj, lb_logits, gain, *, layer, batch, seq, width):
    n = proj.shape[0]
    dk = gain.shape[-1]
    heads = width // dk
    hb = _tile(heads, 16)
    hg = heads // hb
    t = _tile(seq, 512)
    ns = seq // t
    n_layers = lb_logits.shape[0]

    def col(seg):
        return pl.BlockSpec((t, hb * dk), lambda b, h, s, seg=seg: (b * ns + s, seg * hg + h))

    kern = functools.partial(_hgrn_kernel, layer=layer, n_chunks=t // HGRN_CHUNK, n_heads=hb)
    buf = pltpu.VMEM((hb, HGRN_PAD + HGRN_CHUNK, dk), F32)
    cbuf = pltpu.VMEM((hb, HGRN_CHUNK, dk), F32)
    masks = pltpu.VMEM((len(HGRN_LEVEL_BLOCKS), HGRN_CHUNK, HGRN_CHUNK), F32)
    return pl.pallas_call(
        kern,
        out_shape=jax.ShapeDtypeStruct((n, width), BF16),
        grid=(batch, hg, ns),
        in_specs=[pl.BlockSpec((n_layers, hb * dk), lambda b, h, s: (0, h)),
                  pl.BlockSpec((1, dk), lambda b, h, s: (0, 0)),
                  col(0), col(1), col(2), col(3)],
        out_specs=pl.BlockSpec((t, hb * dk), lambda b, h, s: (b * ns + s, h)),
        scratch_shapes=[pltpu.VMEM((hb, dk, dk), F32), buf, buf, buf, cbuf,
                        pltpu.VMEM((hb, HGRN_CHUNK, HGRN_CHUNK), BF16), masks],
        compiler_params=_params("parallel", "parallel", "arbitrary"),
        name="hgrn2",
    )(lb_logits, gain.reshape(1, dk), proj, proj, proj, proj)


def _pool_kernel(u_ref, halo_ref, w_ref, sc_ref, o_ref, *, windows):
    g = pl.program_id(2)
    s = pl.program_id(1)
    t = u_ref.shape[0]
    u = u_ref[...]
    win = jnp.int32(windows[0])
    for gi in range(1, len(windows)):
        win = jnp.where(g == gi, jnp.int32(windows[gi]), win)
    r_i = lax.broadcasted_iota(jnp.int32, (t, POOL_HALO + t), 0)
    c_i = lax.broadcasted_iota(jnp.int32, (t, POOL_HALO + t), 1) - POOL_HALO
    lag = r_i - c_i
    inside = (lag >= 0) & (lag < win) & (c_i + s * t >= 0)
    band = jnp.where(inside, 1.0, 0.0).astype(BF16)
    ext = jnp.concatenate([halo_ref[...].astype(BF16), u.astype(BF16)], axis=0)
    wsum = _dot(band, ext)
    pos = (lax.broadcasted_iota(jnp.int32, (t, 1), 0) + s * t + 1).astype(F32)
    pooled = wsum / jnp.minimum(pos, win.astype(F32)) - u
    mixed = _dot(pooled.astype(BF16), w_ref[...].astype(BF16))
    o_ref[...] = (mixed * sc_ref[...]).astype(o_ref.dtype)


def _pool(proj, w_pool, pool_scale, *, layer, batch, seq, col0):
    n = proj.shape[0]
    _, groups, cg, _ = w_pool.shape
    t = _tile(seq, 512)
    ns = seq // t
    hb = t // POOL_HALO
    assert col0 % cg == 0 and groups == len(POOL_WINDOWS)
    cb = col0 // cg
    kern = functools.partial(_pool_kernel, windows=POOL_WINDOWS)
    return pl.pallas_call(
        kern,
        out_shape=jax.ShapeDtypeStruct((n, groups * cg), BF16),
        grid=(batch, ns, groups),
        in_specs=[pl.BlockSpec((t, cg), lambda b, s, g: (b * ns + s, cb + g)),
                  pl.BlockSpec((POOL_HALO, cg),
                               lambda b, s, g: (jnp.maximum((b * ns + s) * hb - 1, 0), cb + g)),
                  pl.BlockSpec((None, None, cg, cg), lambda b, s, g: (layer, g, 0, 0)),
                  pl.BlockSpec((1, cg), lambda b, s, g: (0, g))],
        out_specs=pl.BlockSpec((t, cg), lambda b, s, g: (b * ns + s, g)),
        compiler_params=_params("parallel", "parallel", "arbitrary"),
        name="pool_mixer",
    )(proj, proj, w_pool, pool_scale[layer].reshape(1, groups * cg))


def _merge_kernel(oa_ref, ob_ref, wa_ref, wb_ref, ga_ref, gb_ref, o_ref):
    a = _dot(oa_ref[...], wa_ref[...].astype(BF16))
    b = _dot(ob_ref[...], wb_ref[...].astype(BF16))
    o_ref[...] = (_sigmoid(ga_ref[...]) * a + _sigmoid(gb_ref[...]) * b).astype(o_ref.dtype)


def _merge(o_a, o_b, w_a, w_b, proj, *, layer, col_ga, col_gb):
    n, wa = o_a.shape
    wb = o_b.shape[1]
    d = w_a.shape[2]
    bm, bn = _tile(n, 1024), _tile(d, 512)
    assert col_ga % bn == 0 and col_gb % bn == 0
    ca, cb = col_ga // bn, col_gb // bn
    return pl.pallas_call(
        _merge_kernel,
        out_shape=jax.ShapeDtypeStruct((n, d), BF16),
        grid=(n // bm, d // bn),
        in_specs=[pl.BlockSpec((bm, wa), lambda i, j: (i, 0)),
                  pl.BlockSpec((bm, wb), lambda i, j: (i, 0)),
                  pl.BlockSpec((None, wa, bn), lambda i, j: (layer, 0, j)),
                  pl.BlockSpec((None, wb, bn), lambda i, j: (layer, 0, j)),
                  pl.BlockSpec((bm, bn), lambda i, j: (i, ca + j)),
                  pl.BlockSpec((bm, bn), lambda i, j: (i, cb + j))],
        out_specs=pl.BlockSpec((bm, bn), lambda i, j: (i, j)),
        compiler_params=_params("parallel", "arbitrary"),
        name="gated_merge",
    )(o_a, o_b, w_a, w_b, proj, proj)


def _outproj_kernel(m_ref, w_ref, x_ref, o_ref):
    o_ref[...] = x_ref[...] + _dot(m_ref[...], w_ref[...].astype(BF16))


def _outproj(merged, w, x, layer):
    n, d = merged.shape
    dn = w.shape[2]
    bm, bn = _tile(n, 1024), _tile(dn, 512)
    return pl.pallas_call(
        _outproj_kernel,
        out_shape=jax.ShapeDtypeStruct((n, dn), F32),
        grid=(n // bm, dn // bn),
        in_specs=[pl.BlockSpec((bm, d), lambda i, j: (i, 0)),
                  pl.BlockSpec((None, d, bn), lambda i, j: (layer, 0, j)),
                  pl.BlockSpec((bm, bn), lambda i, j: (i, j))],
        out_specs=pl.BlockSpec((bm, bn), lambda i, j: (i, j)),
        compiler_params=_params("parallel", "arbitrary"),
        name="outproj_residual",
    )(merged, w, x)


def _router_kernel(x_ref, g_ref, wr_ref, hp_ref, route_ref, cnt_ref, run_ref, *, n_groups, n_experts):
    tr, d = x_ref.shape

    @pl.when(pl.program_id(0) == 0)
    def _():
        run_ref[...] = jnp.zeros_like(run_ref)

    h = _rms(x_ref[...], g_ref[...])
    hb = h.astype(BF16)
    hp_ref[...] = h

    h_lo = (h - hb.astype(F32)).astype(BF16)
    wr = wr_ref[...]
    w_hi = wr.astype(BF16)
    w_lo = (wr - w_hi.astype(F32)).astype(BF16)
    logits = _dot(hb, w_hi) + _dot(hb, w_lo) + _dot(h_lo, w_hi)

    lane = lax.broadcasted_iota(jnp.int32, (tr, LANES), 1).astype(F32)
    ninf = jnp.float32(-jnp.inf)
    big = jnp.float32(1e9)
    is_g = lane < n_groups
    gl = jnp.where(is_g, logits, ninf)
    gmax = jnp.max(gl, axis=-1, keepdims=True)
    gidx = jnp.min(jnp.where(gl == gmax, lane, big), axis=-1, keepdims=True)
    g_w = 1.0 / jnp.sum(jnp.where(is_g, jnp.exp(gl - gmax), 0.0), axis=-1, keepdims=True)

    e_lo = n_groups + gidx * n_experts
    el = jnp.where(lane >= e_lo, jnp.where(lane < e_lo + n_experts, logits, ninf), ninf)
    v1 = jnp.max(el, axis=-1, keepdims=True)
    i1 = jnp.min(jnp.where(el == v1, lane, big), axis=-1, keepdims=True)
    el2 = jnp.where(lane == i1, ninf, el)
    v2 = jnp.max(el2, axis=-1, keepdims=True)
    i2 = jnp.min(jnp.where(el2 == v2, lane, big), axis=-1, keepdims=True)
    tt = jnp.exp(v2 - v1)
    w1 = g_w / (1.0 + tt)
    w2 = g_w * tt / (1.0 + tt)
    e1 = i1 - n_groups
    e2 = i2 - n_groups

    oh1 = lane == e1
    oh2 = lane == e2
    cnt = jnp.where(oh1, 1.0, 0.0) + jnp.where(oh2, 1.0, 0.0)
    r_i = lax.broadcasted_iota(jnp.int32, (tr, tr), 0)
    c_i = lax.broadcasted_iota(jnp.int32, (tr, tr), 1)
    strict = jnp.where(r_i > c_i, 1.0, 0.0).astype(BF16)
    before = _dot(strict, cnt.astype(BF16)) + run_ref[0:1, :]
    rank1 = jnp.sum(jnp.where(oh1, before, 0.0), axis=-1, keepdims=True)
    rank2 = jnp.sum(jnp.where(oh2, before, 0.0), axis=-1, keepdims=True)
    run = run_ref[0:1, :] + jnp.sum(cnt, axis=0, keepdims=True)
    run_ref[...] = jnp.broadcast_to(run, run_ref.shape)
    cnt_ref[...] = jnp.broadcast_to(run, cnt_ref.shape)

    out = jnp.zeros((tr, LANES), F32)
    for idx, val in enumerate((e1, e2, w1, w2, rank1, rank2)):
        out = jnp.where(lane == idx, val, out)
    route_ref[...] = out[:, :ROUTE_COLS]


def _router(x, g, w_router, *, n_groups, n_experts):
    n, d = x.shape
    tr = _tile(n, 256)
    kern = functools.partial(_router_kernel, n_groups=n_groups, n_experts=n_experts)
    return pl.pallas_call(
        kern,
        out_shape=(jax.ShapeDtypeStruct((n, d), F32),
                   jax.ShapeDtypeStruct((n, ROUTE_COLS), F32),
                   jax.ShapeDtypeStruct((8, LANES), F32)),
        grid=(n // tr,),
        in_specs=[pl.BlockSpec((tr, d), lambda i: (i, 0)),
                  pl.BlockSpec((1, d), lambda i: (0, 0)),
                  pl.BlockSpec((d, LANES), lambda i: (0, 0))],
        out_specs=(pl.BlockSpec((tr, d), lambda i: (i, 0)),
                   pl.BlockSpec((tr, ROUTE_COLS), lambda i: (i, 0)),
                   pl.BlockSpec((8, LANES), lambda i: (0, 0))),
        scratch_shapes=[pltpu.VMEM((8, LANES), F32)],
        compiler_params=_params("arbitrary"),
        name="norm_router",
    )(x, g.reshape(1, d), w_router)


def _row_copy(src_ref, src_row, dst_ref, dst_row, sem):
    return pltpu.make_async_copy(src_ref.at[pl.ds(src_row, 1)], dst_ref.at[pl.ds(dst_row, 1)], sem)


def _start_rows(copies, row0, n_rows):
    unroll = DMA_ISSUE_UNROLL

    def issue(g, carry):
        for u in range(unroll):
            for c in copies(row0 + g * unroll + u):
                c.start()
        return carry

    lax.fori_loop(0, n_rows // unroll, issue, 0)


def _wait_rows(copies, row0, n_rows):
    unroll = DMA_ISSUE_UNROLL
    first = copies(row0)

    def drain(g, carry):
        for _ in range(unroll):
            for c in first:
                c.wait()
        return carry

    lax.fori_loop(0, n_rows // unroll, drain, 0)


def _start_then_wait_rows(copies, n_rows):
    _start_rows(copies, 0, n_rows)
    _wait_rows(copies, 0, n_rows)


def _dest_row(code_ref, offs_ref, t):
    code = code_ref[t]
    return offs_ref[code >> ROUTE_CODE_SHIFT] + (code & ((1 << ROUTE_CODE_SHIFT) - 1))


def _dispatch_kernel(c1_ref, c2_ref, offs_ref, zb_ref, zf_ref, h_ref, xs_ref, zero_ref, sem, *, n_exp):
    tr = h_ref.shape[0]
    sub = zero_ref.shape[0]
    base = pl.program_id(0) * tr

    def zero_copy(e):
        dst = xs_ref.at[pl.ds(pl.multiple_of(zb_ref[e] * sub, sub), sub)]
        return pltpu.make_async_copy(zero_ref, dst, sem.at[2])

    @pl.when(pl.program_id(0) == 0)
    def _():
        zero_ref[...] = jnp.zeros_like(zero_ref)

        def z_start(e, carry):
            @pl.when(zf_ref[e] != 0)
            def _():
                zero_copy(e).start()
            return carry

        def z_wait(e, carry):
            @pl.when(zf_ref[e] != 0)
            def _():
                zero_copy(e).wait()
            return carry

        lax.fori_loop(0, n_exp, z_start, 0)
        lax.fori_loop(0, n_exp, z_wait, 0)

    def copies(r):
        t = base + r
        return (_row_copy(h_ref, r, xs_ref, _dest_row(c1_ref, offs_ref, t), sem.at[0]),
                _row_copy(h_ref, r, xs_ref, _dest_row(c2_ref, offs_ref, t), sem.at[1]))

    _start_then_wait_rows(copies, tr)


def _dispatch(hp, code1, code2, offs, zero_block, zero_flag, *, rows, sub):
    n, dw = hp.shape
    tr = _tile(n, 256)
    grid_spec = pltpu.PrefetchScalarGridSpec(
        num_scalar_prefetch=5,
        grid=(n // tr,),
        in_specs=[pl.BlockSpec((tr, dw), lambda i, *_: (i, 0))],
        out_specs=pl.BlockSpec(memory_space=pl.ANY),
        scratch_shapes=[pltpu.VMEM((sub, dw), hp.dtype), pltpu.SemaphoreType.DMA((3,))],
    )
    return pl.pallas_call(
        functools.partial(_dispatch_kernel, n_exp=zero_block.shape[0]),
        out_shape=jax.ShapeDtypeStruct((rows, dw), hp.dtype),
        grid_spec=grid_spec,
        compiler_params=_params("arbitrary"),
        name="dispatch",
    )(code1, code2, offs, zero_block, zero_flag, hp)


def _expert_kernel(se_ref, sb_ref, sn_ref, ns_ref, xs_ref, wg_ref, wu_ref, wd_ref, o_ref,
                   hid_ref, *, n_up, sub):
    del se_ref, sb_ref
    s, p = pl.program_id(0), pl.program_id(1)
    valid = s < ns_ref[0]
    hsub = sub // 2
    n_half = sn_ref[s]
    n_full = n_half // 2
    odd = n_half % 2 == 1
    fc = wg_ref.shape[1]

    def rows_at(q_half, size):
        return pl.ds(pl.multiple_of(q_half * hsub, hsub), size)

    @pl.when(jnp.logical_and(valid, p < n_up))
    def _():
        wg = wg_ref[...].astype(BF16)
        wu = wu_ref[...].astype(BF16)

        def up(rows):
            x = xs_ref[rows, :].astype(BF16)
            a = _dot(x, wg)
            b = _dot(x, wu)
            hid_ref[p, rows, :] = (a * _sigmoid(a) * b).astype(BF16)

        def body(q, carry):
            up(rows_at(2 * q, sub))
            return carry

        lax.fori_loop(0, n_full, body, 0)

        @pl.when(odd)
        def _():
            up(rows_at(2 * n_full, hsub))

    @pl.when(jnp.logical_and(valid, p >= n_up))
    def _():
        wd = wd_ref[...].astype(BF16)

        def down(rows):
            y = _dot(hid_ref[0, rows, :], wd[0:fc])
            for j in range(1, n_up):
                y += _dot(hid_ref[j, rows, :], wd[j * fc:(j + 1) * fc])
            o_ref[rows, :] = y

        def body(q, carry):
            down(rows_at(2 * q, sub))
            return carry

        def zero_body(q, carry):
            o_ref[rows_at(q, hsub), :] = jnp.zeros((hsub, o_ref.shape[1]), o_ref.dtype)
            return carry

        lax.fori_loop(0, n_full, body, 0)

        @pl.when(odd)
        def _():
            down(rows_at(2 * n_full, hsub))

        lax.fori_loop(n_half, xs_ref.shape[0] // hsub, zero_body, 0)


def _experts(xs, w_gate, w_up, w_down, sup_expert, sup_block, sup_nsub, n_super, *, sub, sup):
    rows = xs.shape[0]
    _, d, f = w_gate.shape
    fc = _tile(f, 256)
    n_up = f // fc
    dc = _tile(d, min(1024, d // 2))
    n_down = d // dc
    n_phase = n_up + n_down

    def phase(s, p, ns):
        return jnp.where(s < ns[0], p, n_phase - 1)

    def x_map(s, p, se, sb, sn, ns):
        return (sb[s + (phase(s, p, ns) >= n_up).astype(jnp.int32)], 0)

    def up_map(first_down_phase):
        def index_map(s, p, se, sb, sn, ns):
            ph = phase(s, p, ns)
            ahead = ph >= n_up + first_down_phase
            return (se[s + ahead.astype(jnp.int32)], 0, jnp.where(ahead, 0, jnp.minimum(ph, n_up - 1)))
        return index_map

    def down_c(s, p, ns):
        return jnp.maximum(phase(s, p, ns) - n_up, 0)

    def down_map(s, p, se, sb, sn, ns):
        early = jnp.logical_and(phase(s, p, ns) < n_up - 1, s > 0)
        expert = se[jnp.where(early, s - 1, s)]
        return (expert, 0, jnp.where(early, n_down - 1, down_c(s, p, ns)))

    grid_spec = pltpu.PrefetchScalarGridSpec(
        num_scalar_prefetch=4,
        grid=(rows // sup, n_phase),
        in_specs=[pl.BlockSpec((sup, d), x_map),
                  pl.BlockSpec((None, d, fc), up_map(0)),
                  pl.BlockSpec((None, d, fc), up_map(1)),
                  pl.BlockSpec((None, f, dc), down_map)],
        out_specs=pl.BlockSpec((sup, dc), lambda s, p, se, sb, sn, ns: (sb[s], down_c(s, p, ns))),
        scratch_shapes=[pltpu.VMEM((n_up, sup, fc), BF16)],
    )
    return pl.pallas_call(
        functools.partial(_expert_kernel, n_up=n_up, sub=sub),
        out_shape=jax.ShapeDtypeStruct((rows, d), F32),
        grid_spec=grid_spec,
        compiler_params=_params("arbitrary", "arbitrary"),
        name="expert_mlp",
    )(sup_expert, sup_block, sup_nsub, n_super, xs, w_gate, w_up, w_down)


def _combine_kernel(c1_ref, c2_ref, offs_ref, x_ref, route_ref, g_ref, ys_ref, *rest, final):
    if final:
        y_ref, buf1, buf2, sem = rest
    else:
        x2_ref, hn_ref, buf1, buf2, sem = rest
    tr = x_ref.shape[0]
    step = pl.program_id(0)
    slot = step % 2
    out_ref = y_ref if final else hn_ref

    def copies_of(blk):
        s = blk % 2

        def copies(r):
            t = blk * tr + r
            return (_row_copy(ys_ref, _dest_row(c1_ref, offs_ref, t), buf1.at[s], r, sem.at[s, 0]),
                    _row_copy(ys_ref, _dest_row(c2_ref, offs_ref, t), buf2.at[s], r, sem.at[s, 1]))
        return copies

    @pl.when(step == 0)
    def _():
        _start_rows(copies_of(step), 0, tr)

    @pl.when(step + 1 < pl.num_programs(0))
    def _():
        _start_rows(copies_of(step + 1), 0, tr)

    _wait_rows(copies_of(step), 0, tr)

    route = route_ref[...]
    w1, w2 = route[:, 2:3], route[:, 3:4]
    x2 = x_ref[...] + w1 * buf1[slot] + w2 * buf2[slot]
    if not final:
        x2_ref[...] = x2
    out_ref[...] = _rms(x2, g_ref[...]).astype(out_ref.dtype)


def _combine(x, route, g, ys, code1, code2, offs, *, final):
    n, d = x.shape
    tr = _tile(n, 256)
    row = lambda i, *_: (i, 0)
    if final:
        out_shape = jax.ShapeDtypeStruct((n, d), F32)
        out_specs = pl.BlockSpec((tr, d), row)
    else:
        out_shape = (jax.ShapeDtypeStruct((n, d), F32), jax.ShapeDtypeStruct((n, d), BF16))
        out_specs = (pl.BlockSpec((tr, d), row), pl.BlockSpec((tr, d), row))
    grid_spec = pltpu.PrefetchScalarGridSpec(
        num_scalar_prefetch=3,
        grid=(n // tr,),
        in_specs=[pl.BlockSpec((tr, d), row),
                  pl.BlockSpec((tr, ROUTE_COLS), row),
                  pl.BlockSpec((1, d), lambda i, *_: (0, 0)),
                  pl.BlockSpec(memory_space=pl.ANY)],
        out_specs=out_specs,
        scratch_shapes=[pltpu.VMEM((2, tr, d), F32), pltpu.VMEM((2, tr, d), F32),
                        pltpu.SemaphoreType.DMA((2, 2))],
    )
    return pl.pallas_call(
        functools.partial(_combine_kernel, final=final),
        out_shape=out_shape,
        grid_spec=grid_spec,
        compiler_params=_params("arbitrary"),
        name="combine_final" if final else "combine",
    )(code1, code2, offs, x, route, g.reshape(1, d), ys)


def _plan(route, counts, *, ge, sub, sup, max_super):
    i32 = jnp.int32
    e1, e2 = route[:, 0].astype(i32), route[:, 1].astype(i32)
    rank1, rank2 = route[:, 4].astype(i32), route[:, 5].astype(i32)
    cnt = counts[0, :ge].astype(i32)
    n_sub = (cnt + sub - 1) // sub
    n_sup = (cnt + sup - 1) // sup
    ends = jnp.cumsum(n_sup * sup)
    offs = ends - n_sup * sup
    code1 = e1 * (1 << ROUTE_CODE_SHIFT) + rank1
    code2 = e2 * (1 << ROUTE_CODE_SHIFT) + rank2
    sup_ends = jnp.cumsum(n_sup)
    n_super = sup_ends[-1]
    s = jnp.arange(max_super + 1, dtype=i32)
    se = jnp.minimum(jnp.sum((sup_ends[None, :] <= s[:, None]).astype(i32), axis=1), ge - 1)
    k = s - (sup_ends[se] - n_sup[se])
    sb = offs[se] // sup + k
    sn = jnp.clip(n_sub[se] - k * (sup // sub), 0, sup // sub)
    last = jnp.maximum(n_super - 1, 0)
    valid = s < n_super
    sup_expert = jnp.where(valid, se, se[last]).astype(i32)
    sup_block = jnp.where(valid, sb, sb[last]).astype(i32)
    sup_nsub = jnp.where(valid, sn, 0).astype(i32)
    zero_block = (offs // sub + cnt // sub).astype(i32)
    zero_flag = (cnt % sub != 0).astype(i32)
    return (code1, code2, offs.astype(i32), sup_expert, sup_block, sup_nsub,
            n_super.reshape(1).astype(i32), zero_block, zero_flag)


def kernel(x, norm1_g, w_in, lb_logits, hgrn_norm_g, w_pool, pool_scale, w_branch_a, w_branch_b,
           w_out, norm2_g, w_router_group, w_router_expert, w_gate, w_up, w_down, final_norm_g):
    batch, seq, d = x.shape
    n = batch * seq
    depth = w_in.shape[0]
    width = lb_logits.shape[1]
    pool_width = pool_scale.shape[1]
    n_groups = w_router_group.shape[-1]
    n_experts = w_gate.shape[2]
    ge = n_groups * n_experts
    d_expert = w_gate.shape[-1]
    assert n_groups + ge <= LANES and n < (1 << ROUTE_CODE_SHIFT)
    col_pool = 4 * width
    col_ga = col_pool + pool_width
    col_gb = col_ga + d

    sub = MOE_SUB_ROWS if n >= 4096 else 32
    sup = MOE_HALVES_PER_SUPER * (sub // 2)
    max_super = (TOP_K * n + ge * (sup - 1)) // sup
    rows = max_super * sup

    wg_all = w_gate.reshape(depth * ge, d, d_expert)
    wu_all = w_up.reshape(depth * ge, d, d_expert)
    wd_all = w_down.reshape(depth * ge, d_expert, d)

    xf = x.reshape(n, d)
    h = _rmsnorm(xf, norm1_g[0], BF16)
    for l in range(depth):
        proj = _inproj(h, w_in, l)
        o_a = _hgrn(proj, lb_logits, hgrn_norm_g[l], layer=l, batch=batch, seq=seq, width=width)
        o_b = _pool(proj, w_pool, pool_scale, layer=l, batch=batch, seq=seq, col0=col_pool)
        merged = _merge(o_a, o_b, w_branch_a, w_branch_b, proj, layer=l, col_ga=col_ga, col_gb=col_gb)
        x1 = _outproj(merged, w_out, xf, l)

        w_router = jnp.concatenate([w_router_group[l], w_router_expert[l]], axis=1)
        w_router = jnp.pad(w_router, ((0, 0), (0, LANES - w_router.shape[1])))
        hp, route, counts = _router(x1, norm2_g[l], w_router, n_groups=n_groups, n_experts=n_experts)
        (code1, code2, offs, sup_expert, sup_block, sup_nsub, n_super, zero_block, zero_flag) = _plan(
            route, counts, ge=ge, sub=sub // 2, sup=sup, max_super=max_super)
        xs = _dispatch(hp, code1, code2, offs, zero_block, zero_flag, rows=rows, sub=sub // 2)
        ys = _experts(xs, wg_all, wu_all, wd_all, sup_expert + l * ge, sup_block, sup_nsub, n_super,
                      sub=sub, sup=sup)
        if l + 1 < depth:
            xf, h = _combine(x1, route, norm1_g[l + 1], ys, code1, code2, offs, final=False)
        else:
            out = _combine(x1, route, final_norm_g, ys, code1, code2, offs, final=True)
    return out.reshape(batch, seq, d)
```

```python
import functools

import jax
import jax.numpy as jnp
from jax import lax
from jax.experimental import pallas as pl
from jax.experimental.pallas import tpu as pltpu

RMS_EPS = 1e-6
MIN_FORGET = 1e-30
LOG2E = 1.4426950408889634
POOL_WINDOWS = (2, 4, 8, 16)
POOL_HALO = 16
HGRN_CHUNK = 64
HGRN_PAD = 8
HGRN_LEVEL_BLOCKS = (64, 32, 16, 8, 4)
TOP_K = 2
MOE_SUB_ROWS = 256
MOE_HALVES_PER_SUPER = 5
LANES = 128
ROUTE_COLS = 8
ROUTE_CODE_SHIFT = 20
DMA_ISSUE_UNROLL = 8
VMEM_LIMIT_BYTES = 56 * 1024 * 1024

F32 = jnp.float32
BF16 = jnp.bfloat16


def _params(*sem):
    return pltpu.CompilerParams(dimension_semantics=sem, vmem_limit_bytes=VMEM_LIMIT_BYTES)


def _tile(n, pref):
    t = min(n, pref)
    assert n % t == 0, (n, pref)
    return t


def _sigmoid(z):
    return 1.0 / (1.0 + jnp.exp(-z))


def _dot(a, b):
    return jnp.dot(a, b, preferred_element_type=F32)


def _dot_nt(a, b):
    return lax.dot_general(a, b, (((1,), (1,)), ((), ())), preferred_element_type=F32)


def _rms(x, g):
    return x * lax.rsqrt(jnp.mean(x * x, axis=-1, keepdims=True) + RMS_EPS) * g


def _rmsnorm_kernel(x_ref, g_ref, o_ref):
    o_ref[...] = _rms(x_ref[...], g_ref[...]).astype(o_ref.dtype)


def _rmsnorm(x, g, out_dtype):
    n, d = x.shape
    tr = _tile(n, 256)
    return pl.pallas_call(
        _rmsnorm_kernel,
        out_shape=jax.ShapeDtypeStruct((n, d), out_dtype),
        grid=(n // tr,),
        in_specs=[pl.BlockSpec((tr, d), lambda i: (i, 0)),
                  pl.BlockSpec((1, d), lambda i: (0, 0))],
        out_specs=pl.BlockSpec((tr, d), lambda i: (i, 0)),
        compiler_params=_params("parallel"),
        name="rmsnorm",
    )(x, g.reshape(1, d))


def _inproj_kernel(h_ref, w_ref, o_ref):
    o_ref[...] = _dot(h_ref[...], w_ref[...].astype(BF16))


def _inproj(h, w, layer):
    n, d = h.shape
    cols = w.shape[2]
    bm, bn = _tile(n, 1024), _tile(cols, 768)
    return pl.pallas_call(
        _inproj_kernel,
        out_shape=jax.ShapeDtypeStruct((n, cols), F32),
        grid=(n // bm, cols // bn),
        in_specs=[pl.BlockSpec((bm, d), lambda i, j: (i, 0)),
                  pl.BlockSpec((None, d, bn), lambda i, j: (layer, 0, j))],
        out_specs=pl.BlockSpec((bm, bn), lambda i, j: (i, j)),
        compiler_params=_params("parallel", "arbitrary"),
        name="inproj",
    )(h, w)


def _split3(x):
    hi = x.astype(BF16)
    r1 = x - hi.astype(F32)
    mid = r1.astype(BF16)
    lo = (r1 - mid.astype(F32)).astype(BF16)
    return hi, mid, lo


def _hgrn_kernel(lbl_ref, g_ref, q_ref, f_ref, i_ref, og_ref, o_ref,
                 st_ref, kp_ref, bp_ref, vp_ref, oc_ref, ac_ref, lm_ref, *, layer, n_chunks, n_heads):
    C, PAD = HGRN_CHUNK, HGRN_PAD
    LEVEL_BLOCKS = HGRN_LEVEL_BLOCKS
    dk = g_ref.shape[1]

    @pl.when(pl.program_id(2) == 0)
    def _():
        st_ref[...] = jnp.zeros_like(st_ref)

    zpad = jnp.zeros((PAD, dk), F32)
    for hh in range(n_heads):
        kp_ref[hh, 0:PAD, :] = zpad
        bp_ref[hh, 0:PAD, :] = zpad
        vp_ref[hh, 0:PAD, :] = zpad

    t_i = lax.broadcasted_iota(jnp.int32, (C, C), 0)
    s_i = lax.broadcasted_iota(jnp.int32, (C, C), 1)
    for lvl, blk in enumerate(LEVEL_BLOCKS):
        same = (t_i // blk) == (s_i // blk)
        take = same & ((t_i % blk) >= blk // 2) & ((s_i % blk) < blk // 2)
        lm_ref[lvl] = jnp.where(take, 1.0, 0.0)
    sub8 = lax.broadcasted_iota(jnp.int32, (8, dk), 0)

    lbl = lbl_ref[...]
    e = jnp.exp(lbl - jnp.max(lbl, axis=0, keepdims=True))
    p = e / jnp.sum(e, axis=0, keepdims=True)
    lb_all = jnp.sum(p[0:layer + 1], axis=0, keepdims=True) - p[0:1]
    gain = g_ref[...]

    tri = jnp.where(t_i >= s_i, 1.0, 0.0).astype(BF16)
    tri3 = jnp.concatenate([tri, tri, tri], axis=1)


    def gates(hh, r0):
        rows, cols = pl.ds(r0, C), slice(hh * dk, (hh + 1) * dk)
        lb = lb_all[:, cols]
        z = f_ref[rows, cols]
        gate = (1.0 - lb) * _sigmoid(z)
        f = lb + gate
        lf = jnp.log(jnp.maximum(f, MIN_FORGET))
        k = (1.0 - lb) - gate
        hi, mid, lo = _split3(lf)
        b2 = _dot(tri3, jnp.concatenate([hi, mid, lo], axis=0)) * LOG2E
        kp_ref[hh, PAD:PAD + C, :] = k
        bp_ref[hh, PAD:PAD + C, :] = b2
        vp_ref[hh, PAD:PAD + C, :] = i_ref[rows, cols]

    def scores(hh, r0):
        rows, cols = pl.ds(r0, C), slice(hh * dk, (hh + 1) * dk)
        q = q_ref[rows, cols]
        k, b2 = kp_ref[hh, PAD:PAD + C, :], bp_ref[hh, PAD:PAD + C, :]
        oc_ref[hh] = _dot_nt((q * jnp.exp2(b2)).astype(BF16), st_ref[hh].astype(BF16))

        a = jnp.zeros((C, C), F32)
        for lvl, blk in enumerate(LEVEL_BLOCKS):
            hb_ = blk // 2
            if hb_ >= 8:
                xs_, es_ = [], []
                for b0 in range(0, C, blk):
                    m = bp_ref[hh, PAD + b0 + hb_ - 1:PAD + b0 + hb_, :]
                    xs_ += [k[b0:b0 + hb_], q[b0 + hb_:b0 + blk]]
                    es_ += [m - b2[b0:b0 + hb_], b2[b0 + hb_:b0 + blk] - m]
                x = jnp.concatenate(xs_, axis=0)
                e = jnp.minimum(jnp.concatenate(es_, axis=0), 0.0)
            else:
                up8 = (sub8 & (blk - 1)) >= hb_
                xs_, es_ = [], []
                for g0 in range(0, C, 8):
                    rows_m = [bp_ref[hh, PAD + b0 + hb_ - 1:PAD + b0 + hb_, :] for b0 in range(g0, g0 + 8, blk)]
                    mg = jnp.broadcast_to(rows_m[-1], (8, dk))
                    for j in range(len(rows_m) - 2, -1, -1):
                        mg = jnp.where(sub8 < (j + 1) * blk, rows_m[j], mg)
                    xs_.append(jnp.where(up8, q[g0:g0 + 8], k[g0:g0 + 8]))
                    es_.append(-jnp.abs(b2[g0:g0 + 8] - mg))
                x = jnp.concatenate(xs_, axis=0)
                e = jnp.concatenate(es_, axis=0)
            y = (x * jnp.exp2(e)).astype(BF16)
            a = a + _dot_nt(y, y) * lm_ref[lvl]
        ac_ref[hh] = a.astype(BF16)

    def outputs(hh, r0):
        rows, cols = pl.ds(r0, C), slice(hh * dk, (hh + 1) * dk)
        q, v = q_ref[rows, cols], i_ref[rows, cols]
        k, b2 = kp_ref[hh, PAD:PAD + C, :], bp_ref[hh, PAD:PAD + C, :]
        o = oc_ref[hh] + _dot(ac_ref[hh], v.astype(BF16))
        k1, b1, v1 = kp_ref[hh, PAD - 1:PAD - 1 + C, :], bp_ref[hh, PAD - 1:PAD - 1 + C, :], vp_ref[hh, PAD - 1:PAD - 1 + C, :]
        a1 = jnp.sum(q * k1 * jnp.exp2(jnp.minimum(b2 - b1, 0.0)), axis=-1, keepdims=True)
        a0 = jnp.sum(q * k, axis=-1, keepdims=True)
        odd8 = (sub8 & 1) == 1
        a1 = jnp.concatenate([jnp.where(odd8, a1[g0:g0 + 8], 0.0) for g0 in range(0, C, 8)], axis=0)
        oc_ref[hh] = o + a1 * v1 + a0 * v

    def finish(hh, r0):
        rows, cols = pl.ds(r0, C), slice(hh * dk, (hh + 1) * dk)
        k, b2, v = kp_ref[hh, PAD:PAD + C, :], bp_ref[hh, PAD:PAD + C, :], i_ref[rows, cols]
        b_last = b2[C - 1:C, :]
        khat = k * jnp.exp2(b_last - b2)
        st_ref[hh] = st_ref[hh] * jnp.exp2(b_last) + _dot(v.T.astype(BF16), khat.astype(BF16))
        og = og_ref[rows, cols]
        y = _rms(oc_ref[hh], gain) * (og * _sigmoid(og))
        o_ref[rows, cols] = y.astype(o_ref.dtype)

    def chunk(c, carry):
        r0 = pl.multiple_of(c * C, C)
        for phase in (gates, scores, outputs, finish):
            for hh in range(n_heads):
                phase(hh, r0)
        return carry

    lax.fori_loop(0, n_chunks, chunk, 0)


def _hgrn(proj, lb_logits, gain, *, layer, batch, seq, width):
    n = proj.shape[0]
    dk = gain.shape[-1]
    heads = width // dk
    hb = _tile(heads, 16)
    hg = heads // hb
    t = _tile(seq, 512)
    ns = seq // t
    n_layers = lb_logits.shape[0]

    def col(seg):
        return pl.BlockSpec((t, hb * dk), lambda b, h, s, seg=seg: (b * ns + s, seg * hg + h))

    kern = functools.partial(_hgrn_kernel, layer=layer, n_chunks=t // HGRN_CHUNK, n_heads=hb)
    buf = pltpu.VMEM((hb, HGRN_PAD + HGRN_CHUNK, dk), F32)
    cbuf = pltpu.VMEM((hb, HGRN_CHUNK, dk), F32)
    masks = pltpu.VMEM((len(HGRN_LEVEL_BLOCKS), HGRN_CHUNK, HGRN_CHUNK), F32)
    return pl.pallas_call(
        kern,
        out_shape=jax.ShapeDtypeStruct((n, width), BF16),
        grid=(batch, hg, ns),
        in_specs=[pl.BlockSpec((n_layers, hb * dk), lambda b, h, s: (0, h)),
                  pl.BlockSpec((1, dk), lambda b, h, s: (0, 0)),
                  col(0), col(1), col(2), col(3)],
        out_specs=pl.BlockSpec((t, hb * dk), lambda b, h, s: (b * ns + s, h)),
        scratch_shapes=[pltpu.VMEM((hb, dk, dk), F32), buf, buf, buf, cbuf,
                        pltpu.VMEM((hb, HGRN_CHUNK, HGRN_CHUNK), BF16), masks],
        compiler_params=_params("parallel", "parallel", "arbitrary"),
        name="hgrn2",
    )(lb_logits, gain.reshape(1, dk), proj, proj, proj, proj)


def _pool_kernel(u_ref, halo_ref, w_ref, sc_ref, o_ref, *, windows):
    g = pl.program_id(2)
    s = pl.program_id(1)
    t = u_ref.shape[0]
    u = u_ref[...]
    win = jnp.int32(windows[0])
    for gi in range(1, len(windows)):
        win = jnp.where(g == gi, jnp.int32(windows[gi]), win)
    r_i = lax.broadcasted_iota(jnp.int32, (t, POOL_HALO + t), 0)
    c_i = lax.broadcasted_iota(jnp.int32, (t, POOL_HALO + t), 1) - POOL_HALO
    lag = r_i - c_i
    inside = (lag >= 0) & (lag < win) & (c_i + s * t >= 0)
    band = jnp.where(inside, 1.0, 0.0).astype(BF16)
    ext = jnp.concatenate([halo_ref[...].astype(BF16), u.astype(BF16)], axis=0)
    wsum = _dot(band, ext)
    pos = (lax.broadcasted_iota(jnp.int32, (t, 1), 0) + s * t + 1).astype(F32)
    pooled = wsum / jnp.minimum(pos, win.astype(F32)) - u
    mixed = _dot(pooled.astype(BF16), w_ref[...].astype(BF16))
    o_ref[...] = (mixed * sc_ref[...]).astype(o_ref.dtype)


def _pool(proj, w_pool, pool_scale, *, layer, batch, seq, col0):
    n = proj.shape[0]
    _, groups, cg, _ = w_pool.shape
    t = _tile(seq, 512)
    ns = seq // t
    hb = t // POOL_HALO
    assert col0 % cg == 0 and groups == len(POOL_WINDOWS)
    cb = col0 // cg
    kern = functools.partial(_pool_kernel, windows=POOL_WINDOWS)
    return pl.pallas_call(
        kern,
        out_shape=jax.ShapeDtypeStruct((n, groups * cg), BF16),
        grid=(batch, ns, groups),
        in_specs=[pl.BlockSpec((t, cg), lambda b, s, g: (b * ns + s, cb + g)),
                  pl.BlockSpec((POOL_HALO, cg),
                               lambda b, s, g: (jnp.maximum((b * ns + s) * hb - 1, 0), cb + g)),
                  pl.BlockSpec((None, None, cg, cg), lambda b, s, g: (layer, g, 0, 0)),
                  pl.BlockSpec((1, cg), lambda b, s, g: (0, g))],
        out_specs=pl.BlockSpec((t, cg), lambda b, s, g: (b * ns + s, g)),
        compiler_params=_params("parallel", "parallel", "arbitrary"),
        name="pool_mixer",
    )(proj, proj, w_pool, pool_scale[layer].reshape(1, groups * cg))


def _merge_kernel(oa_ref, ob_ref, wa_ref, wb_ref, ga_ref, gb_ref, o_ref):
    a = _dot(oa_ref[...], wa_ref[...].astype(BF16))
    b = _dot(ob_ref[...], wb_ref[...].astype(BF16))
    o_ref[...] = (_sigmoid(ga_ref[...]) * a + _sigmoid(gb_ref[...]) * b).astype(o_ref.dtype)


def _merge(o_a, o_b, w_a, w_b, proj, *, layer, col_ga, col_gb):
    n, wa = o_a.shape
    wb = o_b.shape[1]
    d = w_a.shape[2]
    bm, bn = _tile(n, 1024), _tile(d, 512)
    assert col_ga % bn == 0 and col_gb % bn == 0
    ca, cb = col_ga // bn, col_gb // bn
    return pl.pallas_call(
        _merge_kernel,
        out_shape=jax.ShapeDtypeStruct((n, d), BF16),
        grid=(n // bm, d // bn),
        in_specs=[pl.BlockSpec((bm, wa), lambda i, j: (i, 0)),
                  pl.BlockSpec((bm, wb), lambda i, j: (i, 0)),
                  pl.BlockSpec((None, wa, bn), lambda i, j: (layer, 0, j)),
                  pl.BlockSpec((None, wb, bn), lambda i, j: (layer, 0, j)),
                  pl.BlockSpec((bm, bn), lambda i, j: (i, ca + j)),
                  pl.BlockSpec((bm, bn), lambda i, j: (i, cb + j))],
        out_specs=pl.BlockSpec((bm, bn), lambda i, j: (i, j)),
        compiler_params=_params("parallel", "arbitrary"),
        name="gated_merge",
    )(o_a, o_b, w_a, w_b, proj, proj)


def _outproj_kernel(m_ref, w_ref, x_ref, o_ref):
    o_ref[...] = x_ref[...] + _dot(m_ref[...], w_ref[...].astype(BF16))


def _outproj(merged, w, x, layer):
    n, d = merged.shape
    dn = w.shape[2]
    bm, bn = _tile(n, 1024), _tile(dn, 512)
    return pl.pallas_call(
        _outproj_kernel,
        out_shape=jax.ShapeDtypeStruct((n, dn), F32),
        grid=(n // bm, dn // bn),
        in_specs=[pl.BlockSpec((bm, d), lambda i, j: (i, 0)),
                  pl.BlockSpec((None, d, bn), lambda i, j: (layer, 0, j)),
                  pl.BlockSpec((bm, bn), lambda i, j: (i, j))],
        out_specs=pl.BlockSpec((bm, bn), lambda i, j: (i, j)),
        compiler_params=_params("parallel", "arbitrary"),
        name="outproj_residual",
    )(merged, w, x)


def _router_kernel(x_ref, g_ref, wr_ref, hp_ref, route_ref, cnt_ref, run_ref, *, n_groups, n_experts):
    tr, d = x_ref.shape

    @pl.when(pl.program_id(0) == 0)
    def _():
        run_ref[...] = jnp.zeros_like(run_ref)

    h = _rms(x_ref[...], g_ref[...])
    hb = h.astype(BF16)
    hp_ref[...] = h

    h_lo = (h - hb.astype(F32)).astype(BF16)
    wr = wr_ref[...]
    w_hi = wr.astype(BF16)
    w_lo = (wr - w_hi.astype(F32)).astype(BF16)
    logits = _dot(hb, w_hi) + _dot(hb, w_lo) + _dot(h_lo, w_hi)

    lane = lax.broadcasted_iota(jnp.int32, (tr, LANES), 1).astype(F32)
    ninf = jnp.float32(-jnp.inf)
    big = jnp.float32(1e9)
    is_g = lane < n_groups
    gl = jnp.where(is_g, logits, ninf)
    gmax = jnp.max(gl, axis=-1, keepdims=True)
    gidx = jnp.min(jnp.where(gl == gmax, lane, big), axis=-1, keepdims=True)
    g_w = 1.0 / jnp.sum(jnp.where(is_g, jnp.exp(gl - gmax), 0.0), axis=-1, keepdims=True)

    e_lo = n_groups + gidx * n_experts
    el = jnp.where(lane >= e_lo, jnp.where(lane < e_lo + n_experts, logits, ninf), ninf)
    v1 = jnp.max(el, axis=-1, keepdims=True)
    i1 = jnp.min(jnp.where(el == v1, lane, big), axis=-1, keepdims=True)
    el2 = jnp.where(lane == i1, ninf, el)
    v2 = jnp.max(el2, axis=-1, keepdims=True)
    i2 = jnp.min(jnp.where(el2 == v2, lane, big), axis=-1, keepdims=True)
    tt = jnp.exp(v2 - v1)
    w1 = g_w / (1.0 + tt)
    w2 = g_w * tt / (1.0 + tt)
    e1 = i1 - n_groups
    e2 = i2 - n_groups

    oh1 = lane == e1
    oh2 = lane == e2
    cnt = jnp.where(oh1, 1.0, 0.0) + jnp.where(oh2, 1.0, 0.0)
    r_i = lax.broadcasted_iota(jnp.int32, (tr, tr), 0)
    c_i = lax.broadcasted_iota(jnp.int32, (tr, tr), 1)
    strict = jnp.where(r_i > c_i, 1.0, 0.0).astype(BF16)
    before = _dot(strict, cnt.astype(BF16)) + run_ref[0:1, :]
    rank1 = jnp.sum(jnp.where(oh1, before, 0.0), axis=-1, keepdims=True)
    rank2 = jnp.sum(jnp.where(oh2, before, 0.0), axis=-1, keepdims=True)
    run = run_ref[0:1, :] + jnp.sum(cnt, axis=0, keepdims=True)
    run_ref[...] = jnp.broadcast_to(run, run_ref.shape)
    cnt_ref[...] = jnp.broadcast_to(run, cnt_ref.shape)

    out = jnp.zeros((tr, LANES), F32)
    for idx, val in enumerate((e1, e2, w1, w2, rank1, rank2)):
        out = jnp.where(lane == idx, val, out)
    route_ref[...] = out[:, :ROUTE_COLS]


def _router(x, g, w_router, *, n_groups, n_experts):
    n, d = x.shape
    tr = _tile(n, 256)
    kern = functools.partial(_router_kernel, n_groups=n_groups, n_experts=n_experts)
    return pl.pallas_call(
        kern,
        out_shape=(jax.ShapeDtypeStruct((n, d), F32),
                   jax.ShapeDtypeStruct((n, ROUTE_COLS), F32),
                   jax.ShapeDtypeStruct((8, LANES), F32)),
        grid=(n // tr,),
        in_specs=[pl.BlockSpec((tr, d), lambda i: (i, 0)),
                  pl.BlockSpec((1, d), lambda i: (0, 0)),
                  pl.BlockSpec((d, LANES), lambda i: (0, 0))],
        out_specs=(pl.BlockSpec((tr, d), lambda i: (i, 0)),
                   pl.BlockSpec((tr, ROUTE_COLS), lambda i: (i, 0)),
                   pl.BlockSpec((8, LANES), lambda i: (0, 0))),
        scratch_shapes=[pltpu.VMEM((8, LANES), F32)],
        compiler_params=_params("arbitrary"),
        name="norm_router",
    )(x, g.reshape(1, d), w_router)


def _row_copy(src_ref, src_row, dst_ref, dst_row, sem):
    return pltpu.make_async_copy(src_ref.at[pl.ds(src_row, 1)], dst_ref.at[pl.ds(dst_row, 1)], sem)


def _start_rows(copies, row0, n_rows):
    unroll = DMA_ISSUE_UNROLL

    def issue(g, carry):
        for u in range(unroll):
            for c in copies(row0 + g * unroll + u):
                c.start()
        return carry

    lax.fori_loop(0, n_rows // unroll, issue, 0)


def _wait_rows(copies, row0, n_rows):
    unroll = DMA_ISSUE_UNROLL
    first = copies(row0)

    def drain(g, carry):
        for _ in range(unroll):
            for c in first:
                c.wait()
        return carry

    lax.fori_loop(0, n_rows // unroll, drain, 0)


def _start_then_wait_rows(copies, n_rows):
    _start_rows(copies, 0, n_rows)
    _wait_rows(copies, 0, n_rows)


def _dest_row(code_ref, offs_ref, t):
    code = code_ref[t]
    return offs_ref[code >> ROUTE_CODE_SHIFT] + (code & ((1 << ROUTE_CODE_SHIFT) - 1))


def _dispatch_kernel(c1_ref, c2_ref, offs_ref, zb_ref, zf_ref, h_ref, xs_ref, zero_ref, sem, *, n_exp):
    tr = h_ref.shape[0]
    sub = zero_ref.shape[0]
    base = pl.program_id(0) * tr

    def zero_copy(e):
        dst = xs_ref.at[pl.ds(pl.multiple_of(zb_ref[e] * sub, sub), sub)]
        return pltpu.make_async_copy(zero_ref, dst, sem.at[2])

    @pl.when(pl.program_id(0) == 0)
    def _():
        zero_ref[...] = jnp.zeros_like(zero_ref)

        def z_start(e, carry):
            @pl.when(zf_ref[e] != 0)
            def _():
                zero_copy(e).start()
            return carry

        def z_wait(e, carry):
            @pl.when(zf_ref[e] != 0)
            def _():
                zero_copy(e).wait()
            return carry

        lax.fori_loop(0, n_exp, z_start, 0)
        lax.fori_loop(0, n_exp, z_wait, 0)

    def copies(r):
        t = base + r
        return (_row_copy(h_ref, r, xs_ref, _dest_row(c1_ref, offs_ref, t), sem.at[0]),
                _row_copy(h_ref, r, xs_ref, _dest_row(c2_ref, offs_ref, t), sem.at[1]))

    _start_then_wait_rows(copies, tr)


def _dispatch(hp, code1, code2, offs, zero_block, zero_flag, *, rows, sub):
    n, dw = hp.shape
    tr = _tile(n, 256)
    grid_spec = pltpu.PrefetchScalarGridSpec(
        num_scalar_prefetch=5,
        grid=(n // tr,),
        in_specs=[pl.BlockSpec((tr, dw), lambda i, *_: (i, 0))],
        out_specs=pl.BlockSpec(memory_space=pl.ANY),
        scratch_shapes=[pltpu.VMEM((sub, dw), hp.dtype), pltpu.SemaphoreType.DMA((3,))],
    )
    return pl.pallas_call(
        functools.partial(_dispatch_kernel, n_exp=zero_block.shape[0]),
        out_shape=jax.ShapeDtypeStruct((rows, dw), hp.dtype),
        grid_spec=grid_spec,
        compiler_params=_params("arbitrary"),
        name="dispatch",
    )(code1, code2, offs, zero_block, zero_flag, hp)


def _expert_kernel(se_ref, sb_ref, sn_ref, ns_ref, xs_ref, wg_ref, wu_ref, wd_ref, o_ref,
                   hid_ref, *, n_up, sub):
    del se_ref, sb_ref
    s, p = pl.program_id(0), pl.program_id(1)
    valid = s < ns_ref[0]
    hsub = sub // 2
    n_half = sn_ref[s]
    n_full = n_half // 2
    odd = n_half % 2 == 1
    fc = wg_ref.shape[1]

    def rows_at(q_half, size):
        return pl.ds(pl.multiple_of(q_half * hsub, hsub), size)

    @pl.when(jnp.logical_and(valid, p < n_up))
    def _():
        wg = wg_ref[...].astype(BF16)
        wu = wu_ref[...].astype(BF16)

        def up(rows):
            x = xs_ref[rows, :].astype(BF16)
            a = _dot(x, wg)
            b = _dot(x, wu)
            hid_ref[p, rows, :] = (a * _sigmoid(a) * b).astype(BF16)

        def body(q, carry):
            up(rows_at(2 * q, sub))
            return carry

        lax.fori_loop(0, n_full, body, 0)

        @pl.when(odd)
        def _():
            up(rows_at(2 * n_full, hsub))

    @pl.when(jnp.logical_and(valid, p >= n_up))
    def _():
        wd = wd_ref[...].astype(BF16)

        def down(rows):
            y = _dot(hid_ref[0, rows, :], wd[0:fc])
            for j in range(1, n_up):
                y += _dot(hid_ref[j, rows, :], wd[j * fc:(j + 1) * fc])
            o_ref[rows, :] = y

        def body(q, carry):
            down(rows_at(2 * q, sub))
            return carry

        def zero_body(q, carry):
            o_ref[rows_at(q, hsub), :] = jnp.zeros((hsub, o_ref.shape[1]), o_ref.dtype)
            return carry

        lax.fori_loop(0, n_full, body, 0)

        @pl.when(odd)
        def _():
            down(rows_at(2 * n_full, hsub))

        lax.fori_loop(n_half, xs_ref.shape[0] // hsub, zero_body, 0)


def _experts(xs, w_gate, w_up, w_down, sup_expert, sup_block, sup_nsub, n_super, *, sub, sup):
    rows = xs.shape[0]
    _, d, f = w_gate.shape
    fc = _tile(f, 256)
    n_up = f // fc
    dc = _tile(d, min(1024, d // 2))
    n_down = d // dc
    n_phase = n_up + n_down

    def phase(s, p, ns):
        return jnp.where(s < ns[0], p, n_phase - 1)

    def x_map(s, p, se, sb, sn, ns):
        return (sb[s + (phase(s, p, ns) >= n_up).astype(jnp.int32)], 0)

    def up_map(first_down_phase):
        def index_map(s, p, se, sb, sn, ns):
            ph = phase(s, p, ns)
            ahead = ph >= n_up + first_down_phase
            return (se[s + ahead.astype(jnp.int32)], 0, jnp.where(ahead, 0, jnp.minimum(ph, n_up - 1)))
        return index_map

    def down_c(s, p, ns):
        return jnp.maximum(phase(s, p, ns) - n_up, 0)

    def down_map(s, p, se, sb, sn, ns):
        early = jnp.logical_and(phase(s, p, ns) < n_up - 1, s > 0)
        expert = se[jnp.where(early, s - 1, s)]
        return (expert, 0, jnp.where(early, n_down - 1, down_c(s, p, ns)))

    grid_spec = pltpu.PrefetchScalarGridSpec(
        num_scalar_prefetch=4,
        grid=(rows // sup, n_phase),
        in_specs=[pl.BlockSpec((sup, d), x_map),
                  pl.BlockSpec((None, d, fc), up_map(0)),
                  pl.BlockSpec((None, d, fc), up_map(1)),
                  pl.BlockSpec((None, f, dc), down_map)],
        out_specs=pl.BlockSpec((sup, dc), lambda s, p, se, sb, sn, ns: (sb[s], down_c(s, p, ns))),
        scratch_shapes=[pltpu.VMEM((n_up, sup, fc), BF16)],
    )
    return pl.pallas_call(
        functools.partial(_expert_kernel, n_up=n_up, sub=sub),
        out_shape=jax.ShapeDtypeStruct((rows, d), F32),
        grid_spec=grid_spec,
        compiler_params=_params("arbitrary", "arbitrary"),
        name="expert_mlp",
    )(sup_expert, sup_block, sup_nsub, n_super, xs, w_gate, w_up, w_down)


def _combine_kernel(c1_ref, c2_ref, offs_ref, x_ref, route_ref, g_ref, ys_ref, *rest, final):
    if final:
        y_ref, buf1, buf2, sem = rest
    else:
        x2_ref, hn_ref, buf1, buf2, sem = rest
    tr = x_ref.shape[0]
    step = pl.program_id(0)
    slot = step % 2
    out_ref = y_ref if final else hn_ref

    def copies_of(blk):
        s = blk % 2

        def copies(r):
            t = blk * tr + r
            return (_row_copy(ys_ref, _dest_row(c1_ref, offs_ref, t), buf1.at[s], r, sem.at[s, 0]),
                    _row_copy(ys_ref, _dest_row(c2_ref, offs_ref, t), buf2.at[s], r, sem.at[s, 1]))
        return copies

    @pl.when(step == 0)
    def _():
        _start_rows(copies_of(step), 0, tr)

    @pl.when(step + 1 < pl.num_programs(0))
    def _():
        _start_rows(copies_of(step + 1), 0, tr)

    _wait_rows(copies_of(step), 0, tr)

    route = route_ref[...]
    w1, w2 = route[:, 2:3], route[:, 3:4]
    x2 = x_ref[...] + w1 * buf1[slot] + w2 * buf2[slot]
    if not final:
        x2_ref[...] = x2
    out_ref[...] = _rms(x2, g_ref[...]).astype(out_ref.dtype)


def _combine(x, route, g, ys, code1, code2, offs, *, final):
    n, d = x.shape
    tr = _tile(n, 256)
    row = lambda i, *_: (i, 0)
    if final:
        out_shape = jax.ShapeDtypeStruct((n, d), F32)
        out_specs = pl.BlockSpec((tr, d), row)
    else:
        out_shape = (jax.ShapeDtypeStruct((n, d), F32), jax.ShapeDtypeStruct((n, d), BF16))
        out_specs = (pl.BlockSpec((tr, d), row), pl.BlockSpec((tr, d), row))
    grid_spec = pltpu.PrefetchScalarGridSpec(
        num_scalar_prefetch=3,
        grid=(n // tr,),
        in_specs=[pl.BlockSpec((tr, d), row),
                  pl.BlockSpec((tr, ROUTE_COLS), row),
                  pl.BlockSpec((1, d), lambda i, *_: (0, 0)),
                  pl.BlockSpec(memory_space=pl.ANY)],
        out_specs=out_specs,
        scratch_shapes=[pltpu.VMEM((2, tr, d), F32), pltpu.VMEM((2, tr, d), F32),
                        pltpu.SemaphoreType.DMA((2, 2))],
    )
    return pl.pallas_call(
        functools.partial(_combine_kernel, final=final),
        out_shape=out_shape,
        grid_spec=grid_spec,
        compiler_params=_params("arbitrary"),
        name="combine_final" if final else "combine",
    )(code1, code2, offs, x, route, g.reshape(1, d), ys)


def _plan(route, counts, *, ge, sub, sup, max_super):
    i32 = jnp.int32
    e1, e2 = route[:, 0].astype(i32), route[:, 1].astype(i32)
    rank1, rank2 = route[:, 4].astype(i32), route[:, 5].astype(i32)
    cnt = counts[0, :ge].astype(i32)
    n_sub = (cnt + sub - 1) // sub
    n_sup = (cnt + sup - 1) // sup
    ends = jnp.cumsum(n_sup * sup)
    offs = ends - n_sup * sup
    code1 = e1 * (1 << ROUTE_CODE_SHIFT) + rank1
    code2 = e2 * (1 << ROUTE_CODE_SHIFT) + rank2
    sup_ends = jnp.cumsum(n_sup)
    n_super = sup_ends[-1]
    s = jnp.arange(max_super + 1, dtype=i32)
    se = jnp.minimum(jnp.sum((sup_ends[None, :] <= s[:, None]).astype(i32), axis=1), ge - 1)
    k = s - (sup_ends[se] - n_sup[se])
    sb = offs[se] // sup + k
    sn = jnp.clip(n_sub[se] - k * (sup // sub), 0, sup // sub)
    last = jnp.maximum(n_super - 1, 0)
    valid = s < n_super
    sup_expert = jnp.where(valid, se, se[last]).astype(i32)
    sup_block = jnp.where(valid, sb, sb[last]).astype(i32)
    sup_nsub = jnp.where(valid, sn, 0).astype(i32)
    zero_block = (offs // sub + cnt // sub).astype(i32)
    zero_flag = (cnt % sub != 0).astype(i32)
    return (code1, code2, offs.astype(i32), sup_expert, sup_block, sup_nsub,
            n_super.reshape(1).astype(i32), zero_block, zero_flag)


def kernel(x, norm1_g, w_in, lb_logits, hgrn_norm_g, w_pool, pool_scale, w_branch_a, w_branch_b,
           w_out, norm2_g, w_router_group, w_router_expert, w_gate, w_up, w_down, final_norm_g):
    batch, seq, d = x.shape
    n = batch * seq
    depth = w_in.shape[0]
    width = lb_logits.shape[1]
    pool_width = pool_scale.shape[1]
    n_groups = w_router_group.shape[-1]
    n_experts = w_gate.shape[2]
    ge = n_groups * n_experts
    d_expert = w_gate.shape[-1]
    assert n_groups + ge <= LANES and n < (1 << ROUTE_CODE_SHIFT)
    col_pool = 4 * width
    col_ga = col_pool + pool_width
    col_gb = col_ga + d

    sub = MOE_SUB_ROWS if n >= 4096 else 32
    sup = MOE_HALVES_PER_SUPER * (sub // 2)
    max_super = (TOP_K * n + ge * (sup - 1)) // sup
    rows = max_super * sup

    wg_all = w_gate.reshape(depth * ge, d, d_expert)
    wu_all = w_up.reshape(depth * ge, d, d_expert)
    wd_all = w_down.reshape(depth * ge, d_expert, d)

    xf = x.reshape(n, d)
    h = _rmsnorm(xf, norm1_g[0], BF16)
    for l in range(depth):
        proj = _inproj(h, w_in, l)
        o_a = _hgrn(proj, lb_logits, hgrn_norm_g[l], layer=l, batch=batch, seq=seq, width=width)
        o_b = _pool(proj, w_pool, pool_scale, layer=l, batch=batch, seq=seq, col0=col_pool)
        merged = _merge(o_a, o_b, w_branch_a, w_branch_b, proj, layer=l, col_ga=col_ga, col_gb=col_gb)
        x1 = _outproj(merged, w_out, xf, l)

        w_router = jnp.concatenate([w_router_group[l], w_router_expert[l]], axis=1)
        w_router = jnp.pad(w_router, ((0, 0), (0, LANES - w_router.shape[1])))
        hp, route, counts = _router(x1, norm2_g[l], w_router, n_groups=n_groups, n_experts=n_experts)
        (code1, code2, offs, sup_expert, sup_block, sup_nsub, n_super, zero_block, zero_flag) = _plan(
            route, counts, ge=ge, sub=sub // 2, sup=sup, max_super=max_super)
        xs = _dispatch(hp, code1, code2, offs, zero_block, zero_flag, rows=rows, sub=sub // 2)
        ys = _experts(xs, wg_all, wu_all, wd_all, sup_expert + l * ge, sup_block, sup_nsub, n_super,
                      sub=sub, sup=sup)
        if l + 1 < depth:
            xf, h = _combine(x1, route, norm1_g[l + 1], ys, code1, code2, offs, final=False)
        else:
            out = _combine(x1, route, final_norm_g, ys, code1, code2, offs, final=True)
    return out.reshape(batch, seq, d)
```

```python
import functools

import jax
import jax.numpy as jnp
from jax import lax
from jax.experimental import pallas as pl
from jax.experimental.pallas import tpu as pltpu

RMS_EPS = 1e-6
MIN_FORGET = 1e-30
LOG2E = 1.4426950408889634
POOL_WINDOWS = (2, 4, 8, 16)
POOL_HALO = 16
HGRN_CHUNK = 64
HGRN_PAD = 8
HGRN_LEVEL_BLOCKS = (64, 32, 16, 8, 4)
TOP_K = 2
MOE_SUB_ROWS = 256
MOE_HALVES_PER_SUPER = 5
LANES = 128
ROUTE_COLS = 8
ROUTE_CODE_SHIFT = 20
DMA_ISSUE_UNROLL = 8
VMEM_LIMIT_BYTES = 56 * 1024 * 1024

F32 = jnp.float32
BF16 = jnp.bfloat16


def _params(*sem):
    return pltpu.CompilerParams(dimension_semantics=sem, vmem_limit_bytes=VMEM_LIMIT_BYTES)


def _tile(n, pref):
    t = min(n, pref)
    assert n % t == 0, (n, pref)
    return t


def _sigmoid(z):
    return 1.0 / (1.0 + jnp.exp(-z))


def _dot(a, b):
    return jnp.dot(a, b, preferred_element_type=F32)


def _dot_nt(a, b):
    return lax.dot_general(a, b, (((1,), (1,)), ((), ())), preferred_element_type=F32)


def _rms(x, g):
    return x * lax.rsqrt(jnp.mean(x * x, axis=-1, keepdims=True) + RMS_EPS) * g


def _rmsnorm_kernel(x_ref, g_ref, o_ref):
    o_ref[...] = _rms(x_ref[...], g_ref[...]).astype(o_ref.dtype)


def _rmsnorm(x, g, out_dtype):
    n, d = x.shape
    tr = _tile(n, 256)
    return pl.pallas_call(
        _rmsnorm_kernel,
        out_shape=jax.ShapeDtypeStruct((n, d), out_dtype),
        grid=(n // tr,),
        in_specs=[pl.BlockSpec((tr, d), lambda i: (i, 0)),
                  pl.BlockSpec((1, d), lambda i: (0, 0))],
        out_specs=pl.BlockSpec((tr, d), lambda i: (i, 0)),
        compiler_params=_params("parallel"),
        name="rmsnorm",
    )(x, g.reshape(1, d))


def _inproj_kernel(h_ref, w_ref, o_ref):
    o_ref[...] = _dot(h_ref[...], w_ref[...].astype(BF16))


def _inproj(h, w, layer):
    n, d = h.shape
    cols = w.shape[2]
    bm, bn = _tile(n, 1024), _tile(cols, 768)
    return pl.pallas_call(
        _inproj_kernel,
        out_shape=jax.ShapeDtypeStruct((n, cols), F32),
        grid=(n // bm, cols // bn),
        in_specs=[pl.BlockSpec((bm, d), lambda i, j: (i, 0)),
                  pl.BlockSpec((None, d, bn), lambda i, j: (layer, 0, j))],
        out_specs=pl.BlockSpec((bm, bn), lambda i, j: (i, j)),
        compiler_params=_params("parallel", "arbitrary"),
        name="inproj",
    )(h, w)


def _split3(x):
    hi = x.astype(BF16)
    r1 = x - hi.astype(F32)
    mid = r1.astype(BF16)
    lo = (r1 - mid.astype(F32)).astype(BF16)
    return hi, mid, lo


def _hgrn_kernel(lbl_ref, g_ref, q_ref, f_ref, i_ref, og_ref, o_ref,
                 st_ref, kp_ref, bp_ref, vp_ref, oc_ref, ac_ref, lm_ref, *, layer, n_chunks, n_heads):
    C, PAD = HGRN_CHUNK, HGRN_PAD
    LEVEL_BLOCKS = HGRN_LEVEL_BLOCKS
    dk = g_ref.shape[1]

    @pl.when(pl.program_id(2) == 0)
    def _():
        st_ref[...] = jnp.zeros_like(st_ref)

    zpad = jnp.zeros((PAD, dk), F32)
    for hh in range(n_heads):
        kp_ref[hh, 0:PAD, :] = zpad
        bp_ref[hh, 0:PAD, :] = zpad
        vp_ref[hh, 0:PAD, :] = zpad

    t_i = lax.broadcasted_iota(jnp.int32, (C, C), 0)
    s_i = lax.broadcasted_iota(jnp.int32, (C, C), 1)
    for lvl, blk in enumerate(LEVEL_BLOCKS):
        same = (t_i // blk) == (s_i // blk)
        take = same & ((t_i % blk) >= blk // 2) & ((s_i % blk) < blk // 2)
        lm_ref[lvl] = jnp.where(take, 1.0, 0.0)
    sub8 = lax.broadcasted_iota(jnp.int32, (8, dk), 0)

    lbl = lbl_ref[...]
    e = jnp.exp(lbl - jnp.max(lbl, axis=0, keepdims=True))
    p = e / jnp.sum(e, axis=0, keepdims=True)
    lb_all = jnp.sum(p[0:layer + 1], axis=0, keepdims=True) - p[0:1]
    gain = g_ref[...]

    tri = jnp.where(t_i >= s_i, 1.0, 0.0).astype(BF16)
    tri3 = jnp.concatenate([tri, tri, tri], axis=1)


    def gates(hh, r0):
        rows, cols = pl.ds(r0, C), slice(hh * dk, (hh + 1) * dk)
        lb = lb_all[:, cols]
        z = f_ref[rows, cols]
        gate = (1.0 - lb) * _sigmoid(z)
        f = lb + gate
        lf = jnp.log(jnp.maximum(f, MIN_FORGET))
        k = (1.0 - lb) - gate
        hi, mid, lo = _split3(lf)
        b2 = _dot(tri3, jnp.concatenate([hi, mid, lo], axis=0)) * LOG2E
        kp_ref[hh, PAD:PAD + C, :] = k
        bp_ref[hh, PAD:PAD + C, :] = b2
        vp_ref[hh, PAD:PAD + C, :] = i_ref[rows, cols]

    def scores(hh, r0):
        rows, cols = pl.ds(r0, C), slice(hh * dk, (hh + 1) * dk)
        q = q_ref[rows, cols]
        k, b2 = kp_ref[hh, PAD:PAD + C, :], bp_ref[hh, PAD:PAD + C, :]
        oc_ref[hh] = _dot_nt((q * jnp.exp2(b2)).astype(BF16), st_ref[hh].astype(BF16))

        a = jnp.zeros((C, C), F32)
        for lvl, blk in enumerate(LEVEL_BLOCKS):
            hb_ = blk // 2
            if hb_ >= 8:
                xs_, es_ = [], []
                for b0 in range(0, C, blk):
                    m = bp_ref[hh, PAD + b0 + hb_ - 1:PAD + b0 + hb_, :]
                    xs_ += [k[b0:b0 + hb_], q[b0 + hb_:b0 + blk]]
                    es_ += [m - b2[b0:b0 + hb_], b2[b0 + hb_:b0 + blk] - m]
                x = jnp.concatenate(xs_, axis=0)
                e = jnp.minimum(jnp.concatenate(es_, axis=0), 0.0)
            else:
                up8 = (sub8 & (blk - 1)) >= hb_
                xs_, es_ = [], []
                for g0 in range(0, C, 8):
                    rows_m = [bp_ref[hh, PAD + b0 + hb_ - 1:PAD + b0 + hb_, :] for b0 in range(g0, g0 + 8, blk)]
                    mg = jnp.broadcast_to(rows_m[-1], (8, dk))
                    for j in range(len(rows_m) - 2, -1, -1):
                        mg = jnp.where(sub8 < (j + 1) * blk, rows_m[j], mg)
                    xs_.append(jnp.where(up8, q[g0:g0 + 8], k[g0:g0 + 8]))
                    es_.append(-jnp.abs(b2[g0:g0 + 8] - mg))
                x = jnp.concatenate(xs_, axis=0)
                e = jnp.concatenate(es_, axis=0)
            y = (x * jnp.exp2(e)).astype(BF16)
            a = a + _dot_nt(y, y) * lm_ref[lvl]
        ac_ref[hh] = a.astype(BF16)

    def outputs(hh, r0):
        rows, cols = pl.ds(r0, C), slice(hh * dk, (hh + 1) * dk)
        q, v = q_ref[rows, cols], i_ref[rows, cols]
        k, b2 = kp_ref[hh, PAD:PAD + C, :], bp_ref[hh, PAD:PAD + C, :]
        o = oc_ref[hh] + _dot(ac_ref[hh], v.astype(BF16))
        k1, b1, v1 = kp_ref[hh, PAD - 1:PAD - 1 + C, :], bp_ref[hh, PAD - 1:PAD - 1 + C, :], vp_ref[hh, PAD - 1:PAD - 1 + C, :]
        a1 = jnp.sum(q * k1 * jnp.exp2(jnp.minimum(b2 - b1, 0.0)), axis=-1, keepdims=True)
        a0 = jnp.sum(q * k, axis=-1, keepdims=True)
        odd8 = (sub8 & 1) == 1
        a1 = jnp.concatenate([jnp.where(odd8, a1[g0:g0 + 8], 0.0) for g0 in range(0, C, 8)], axis=0)
        oc_ref[hh] = o + a1 * v1 + a0 * v

    def finish(hh, r0):
        rows, cols = pl.ds(r0, C), slice(hh * dk, (hh + 1) * dk)
        k, b2, v = kp_ref[hh, PAD:PAD + C, :], bp_ref[hh, PAD:PAD + C, :], i_ref[rows, cols]
        b_last = b2[C - 1:C, :]
        khat = k * jnp.exp2(b_last - b2)
        st_ref[hh] = st_ref[hh] * jnp.exp2(b_last) + _dot(v.T.astype(BF16), khat.astype(BF16))
        og = og_ref[rows, cols]
        y = _rms(oc_ref[hh], gain) * (og * _sigmoid(og))
        o_ref[rows, cols] = y.astype(o_ref.dtype)

    def chunk(c, carry):
        r0 = pl.multiple_of(c * C, C)
        for phase in (gates, scores, outputs, finish):
            for hh in range(n_heads):
                phase(hh, r0)
        return carry

    lax.fori_loop(0, n_chunks, chunk, 0)


def _hgrn(proj, lb_logits, gain, *, layer, batch, seq, width):
    n = proj.shape[0]
    dk = gain.shape[-1]
    heads = width // dk
    hb = _tile(heads, 16)
    hg = heads // hb
    t = _tile(seq, 512)
    ns = seq // t
    n_layers = lb_logits.shape[0]

    def col(seg):
        return pl.BlockSpec((t, hb * dk), lambda b, h, s, seg=seg: (b * ns + s, seg * hg + h))

    kern = functools.partial(_hgrn_kernel, layer=layer, n_chunks=t // HGRN_CHUNK, n_heads=hb)
    buf = pltpu.VMEM((hb, HGRN_PAD + HGRN_CHUNK, dk), F32)
    cbuf = pltpu.VMEM((hb, HGRN_CHUNK, dk), F32)
    masks = pltpu.VMEM((len(HGRN_LEVEL_BLOCKS), HGRN_CHUNK, HGRN_CHUNK), F32)
    return pl.pallas_call(
        kern,
        out_shape=jax.ShapeDtypeStruct((n, width), BF16),
        grid=(batch, hg, ns),
        in_specs=[pl.BlockSpec((n_layers, hb * dk), lambda b, h, s: (0, h)),
                  pl.BlockSpec((1, dk), lambda b, h, s: (0, 0)),
                  col(0), col(1), col(2), col(3)],
        out_specs=pl.BlockSpec((t, hb * dk), lambda b, h, s: (b * ns + s, h)),
        scratch_shapes=[pltpu.VMEM((hb, dk, dk), F32), buf, buf, buf, cbuf,
                        pltpu.VMEM((hb, HGRN_CHUNK, HGRN_CHUNK), BF16), masks],
        compiler_params=_params("parallel", "parallel", "arbitrary"),
        name="hgrn2",
    )(lb_logits, gain.reshape(1, dk), proj, proj, proj, proj)


def _pool_kernel(u_ref, halo_ref, w_ref, sc_ref, o_ref, *, windows):
    g = pl.program_id(2)
    s = pl.program_id(1)
    t = u_ref.shape[0]
    u = u_ref[...]
    win = jnp.int32(windows[0])
    for gi in range(1, len(windows)):
        win = jnp.where(g == gi, jnp.int32(windows[gi]), win)
    r_i = lax.broadcasted_iota(jnp.int32, (t, POOL_HALO + t), 0)
    c_i = lax.broadcasted_iota(jnp.int32, (t, POOL_HALO + t), 1) - POOL_HALO
    lag = r_i - c_i
    inside = (lag >= 0) & (lag < win) & (c_i + s * t >= 0)
    band = jnp.where(inside, 1.0, 0.0).astype(BF16)
    ext = jnp.concatenate([halo_ref[...].astype(BF16), u.astype(BF16)], axis=0)
    wsum = _dot(band, ext)
    pos = (lax.broadcasted_iota(jnp.int32, (t, 1), 0) + s * t + 1).astype(F32)
    pooled = wsum / jnp.minimum(pos, win.astype(F32)) - u
    mixed = _dot(pooled.astype(BF16), w_ref[...].astype(BF16))
    o_ref[...] = (mixed * sc_ref[...]).astype(o_ref.dtype)


def _pool(proj, w_pool, pool_scale, *, layer, batch, seq, col0):
    n = proj.shape[0]
    _, groups, cg, _ = w_pool.shape
    t = _tile(seq, 512)
    ns = seq // t
    hb = t // POOL_HALO
    assert col0 % cg == 0 and groups == len(POOL_WINDOWS)
    cb = col0 // cg
    kern = functools.partial(_pool_kernel, windows=POOL_WINDOWS)
    return pl.pallas_call(
        kern,
        out_shape=jax.ShapeDtypeStruct((n, groups * cg), BF16),
        grid=(batch, ns, groups),
        in_specs=[pl.BlockSpec((t, cg), lambda b, s, g: (b * ns + s, cb + g)),
                  pl.BlockSpec((POOL_HALO, cg),
                               lambda b, s, g: (jnp.maximum((b * ns + s) * hb - 1, 0), cb + g)),
                  pl.BlockSpec((None, None, cg, cg), lambda b, s, g: (layer, g, 0, 0)),
                  pl.BlockSpec((1, cg), lambda b, s, g: (0, g))],
        out_specs=pl.BlockSpec((t, cg), lambda b, s, g: (b * ns + s, g)),
        compiler_params=_params("parallel", "parallel", "arbitrary"),
        name="pool_mixer",
    )(proj, proj, w_pool, pool_scale[layer].reshape(1, groups * cg))


def _merge_kernel(oa_ref, ob_ref, wa_ref, wb_ref, ga_ref, gb_ref, o_ref):
    a = _dot(oa_ref[...], wa_ref[...].astype(BF16))
    b = _dot(ob_ref[...], wb_ref[...].astype(BF16))
    o_ref[...] = (_sigmoid(ga_ref[...]) * a + _sigmoid(gb_ref[...]) * b).astype(o_ref.dtype)


def _merge(o_a, o_b, w_a, w_b, proj, *, layer, col_ga, col_gb):
    n, wa = o_a.shape
    wb = o_b.shape[1]
    d = w_a.shape[2]
    bm, bn = _tile(n, 1024), _tile(d, 512)
    assert col_ga % bn == 0 and col_gb % bn == 0
    ca, cb = col_ga // bn, col_gb // bn
    return pl.pallas_call(
        _merge_kernel,
        out_shape=jax.ShapeDtypeStruct((n, d), BF16),
        grid=(n // bm, d // bn),
        in_specs=[pl.BlockSpec((bm, wa), lambda i, j: (i, 0)),
                  pl.BlockSpec((bm, wb), lambda i, j: (i, 0)),
                  pl.BlockSpec((None, wa, bn), lambda i, j: (layer, 0, j)),
                  pl.BlockSpec((None, wb, bn), lambda i, j: (layer, 0, j)),
                  pl.BlockSpec((bm, bn), lambda i, j: (i, ca + j)),
                  pl.BlockSpec((bm, bn), lambda i, j: (i, cb + j))],
        out_specs=pl.BlockSpec((bm, bn), lambda i, j: (i, j)),
        compiler_params=_params("parallel", "arbitrary"),
        name="gated_merge",
    )(o_a, o_b, w_a, w_b, proj, proj)


def _outproj_kernel(m_ref, w_ref, x_ref, o_ref):
    o_ref[...] = x_ref[...] + _dot(m_ref[...], w_ref[...].astype(BF16))


def _outproj(merged, w, x, layer):
    n, d = merged.shape
    dn = w.shape[2]
    bm, bn = _tile(n, 1024), _tile(dn, 512)
    return pl.pallas_call(
        _outproj_kernel,
        out_shape=jax.ShapeDtypeStruct((n, dn), F32),
        grid=(n // bm, dn // bn),
        in_specs=[pl.BlockSpec((bm, d), lambda i, j: (i, 0)),
                  pl.BlockSpec((None, d, bn), lambda i, j: (layer, 0, j)),
                  pl.BlockSpec((bm, bn), lambda i, j: (i, j))],
        out_specs=pl.BlockSpec((bm, bn), lambda i, j: (i, j)),
        compiler_params=_params("parallel", "arbitrary"),
        name="outproj_residual",
    )(merged, w, x)


def _router_kernel(x_ref, g_ref, wr_ref, hp_ref, route_ref, cnt_ref, run_ref, *, n_groups, n_experts):
    tr, d = x_ref.shape

    @pl.when(pl.program_id(0) == 0)
    def _():
        run_ref[...] = jnp.zeros_like(run_ref)

    h = _rms(x_ref[...], g_ref[...])
    hb = h.astype(BF16)
    hp_ref[...] = h

    h_lo = (h - hb.astype(F32)).astype(BF16)
    wr = wr_ref[...]
    w_hi = wr.astype(BF16)
    w_lo = (wr - w_hi.astype(F32)).astype(BF16)
    logits = _dot(hb, w_hi) + _dot(hb, w_lo) + _dot(h_lo, w_hi)

    lane = lax.broadcasted_iota(jnp.int32, (tr, LANES), 1).astype(F32)
    ninf = jnp.float32(-jnp.inf)
    big = jnp.float32(1e9)
    is_g = lane < n_groups
    gl = jnp.where(is_g, logits, ninf)
    gmax = jnp.max(gl, axis=-1, keepdims=True)
    gidx = jnp.min(jnp.where(gl == gmax, lane, big), axis=-1, keepdims=True)
    g_w = 1.0 / jnp.sum(jnp.where(is_g, jnp.exp(gl - gmax), 0.0), axis=-1, keepdims=True)

    e_lo = n_groups + gidx * n_experts
    el = jnp.where(lane >= e_lo, jnp.where(lane < e_lo + n_experts, logits, ninf), ninf)
    v1 = jnp.max(el, axis=-1, keepdims=True)
    i1 = jnp.min(jnp.where(el == v1, lane, big), axis=-1, keepdims=True)
    el2 = jnp.where(lane == i1, ninf, el)
    v2 = jnp.max(el2, axis=-1, keepdims=True)
    i2 = jnp.min(jnp.where(el2 == v2, lane, big), axis=-1, keepdims=True)
    tt = jnp.exp(v2 - v1)
    w1 = g_w / (1.0 + tt)
    w2 = g_w * tt / (1.0 + tt)
    e1 = i1 - n_groups
    e2 = i2 - n_groups

    oh1 = lane == e1
    oh2 = lane == e2
    cnt = jnp.where(oh1, 1.0, 0.0) + jnp.where(oh2, 1.0, 0.0)
    r_i = lax.broadcasted_iota(jnp.int32, (tr, tr), 0)
    c_i = lax.broadcasted_iota(jnp.int32, (tr, tr), 1)
    strict = jnp.where(r_i > c_i, 1.0, 0.0).astype(BF16)
    before = _dot(strict, cnt.astype(BF16)) + run_ref[0:1, :]
    rank1 = jnp.sum(jnp.where(oh1, before, 0.0), axis=-1, keepdims=True)
    rank2 = jnp.sum(jnp.where(oh2, before, 0.0), axis=-1, keepdims=True)
    run = run_ref[0:1, :] + jnp.sum(cnt, axis=0, keepdims=True)
    run_ref[...] = jnp.broadcast_to(run, run_ref.shape)
    cnt_ref[...] = jnp.broadcast_to(run, cnt_ref.shape)

    out = jnp.zeros((tr, LANES), F32)
    for idx, val in enumerate((e1, e2, w1, w2, rank1, rank2)):
        out = jnp.where(lane == idx, val, out)
    route_ref[...] = out[:, :ROUTE_COLS]


def _router(x, g, w_router, *, n_groups, n_experts):
    n, d = x.shape
    tr = _tile(n, 256)
    kern = functools.partial(_router_kernel, n_groups=n_groups, n_experts=n_experts)
    return pl.pallas_call(
        kern,
        out_shape=(jax.ShapeDtypeStruct((n, d), F32),
                   jax.ShapeDtypeStruct((n, ROUTE_COLS), F32),
                   jax.ShapeDtypeStruct((8, LANES), F32)),
        grid=(n // tr,),
        in_specs=[pl.BlockSpec((tr, d), lambda i: (i, 0)),
                  pl.BlockSpec((1, d), lambda i: (0, 0)),
                  pl.BlockSpec((d, LANES), lambda i: (0, 0))],
        out_specs=(pl.BlockSpec((tr, d), lambda i: (i, 0)),
                   pl.BlockSpec((tr, ROUTE_COLS), lambda i: (i, 0)),
                   pl.BlockSpec((8, LANES), lambda i: (0, 0))),
        scratch_shapes=[pltpu.VMEM((8, LANES), F32)],
        compiler_params=_params("arbitrary"),
        name="norm_router",
    )(x, g.reshape(1, d), w_router)


def _row_copy(src_ref, src_row, dst_ref, dst_row, sem):
    return pltpu.make_async_copy(src_ref.at[pl.ds(src_row, 1)], dst_ref.at[pl.ds(dst_row, 1)], sem)


def _start_rows(copies, row0, n_rows):
    unroll = DMA_ISSUE_UNROLL

    def issue(g, carry):
        for u in range(unroll):
            for i, c in enumerate(copies(row0 + g * unroll + u)):
                c.start(priority=i % 2)
        return carry

    lax.fori_loop(0, n_rows // unroll, issue, 0)


def _wait_rows(copies, row0, n_rows):
    unroll = DMA_ISSUE_UNROLL
    first = copies(row0)

    def drain(g, carry):
        for _ in range(unroll):
            for c in first:
                c.wait()
        return carry

    lax.fori_loop(0, n_rows // unroll, drain, 0)


def _start_then_wait_rows(copies, n_rows):
    _start_rows(copies, 0, n_rows)
    _wait_rows(copies, 0, n_rows)


def _dest_row(code_ref, offs_ref, t):
    code = code_ref[t]
    return offs_ref[code >> ROUTE_CODE_SHIFT] + (code & ((1 << ROUTE_CODE_SHIFT) - 1))


def _dispatch_kernel(c1_ref, c2_ref, offs_ref, zb_ref, zf_ref, h_ref, xs_ref, zero_ref, sem, *, n_exp):
    tr = h_ref.shape[0]
    sub = zero_ref.shape[0]
    base = pl.program_id(0) * tr

    def zero_copy(e):
        dst = xs_ref.at[pl.ds(pl.multiple_of(zb_ref[e] * sub, sub), sub)]
        return pltpu.make_async_copy(zero_ref, dst, sem.at[2])

    @pl.when(pl.program_id(0) == 0)
    def _():
        zero_ref[...] = jnp.zeros_like(zero_ref)

        def z_start(e, carry):
            @pl.when(zf_ref[e] != 0)
            def _():
                zero_copy(e).start()
            return carry

        def z_wait(e, carry):
            @pl.when(zf_ref[e] != 0)
            def _():
                zero_copy(e).wait()
            return carry

        lax.fori_loop(0, n_exp, z_start, 0)
        lax.fori_loop(0, n_exp, z_wait, 0)

    def copies(r):
        t = base + r
        return (_row_copy(h_ref, r, xs_ref, _dest_row(c1_ref, offs_ref, t), sem.at[0]),
                _row_copy(h_ref, r, xs_ref, _dest_row(c2_ref, offs_ref, t), sem.at[1]))

    _start_then_wait_rows(copies, tr)


def _dispatch(hp, code1, code2, offs, zero_block, zero_flag, *, rows, sub):
    n, dw = hp.shape
    tr = _tile(n, 256)
    grid_spec = pltpu.PrefetchScalarGridSpec(
        num_scalar_prefetch=5,
        grid=(n // tr,),
        in_specs=[pl.BlockSpec((tr, dw), lambda i, *_: (i, 0))],
        out_specs=pl.BlockSpec(memory_space=pl.ANY),
        scratch_shapes=[pltpu.VMEM((sub, dw), hp.dtype), pltpu.SemaphoreType.DMA((3,))],
    )
    return pl.pallas_call(
        functools.partial(_dispatch_kernel, n_exp=zero_block.shape[0]),
        out_shape=jax.ShapeDtypeStruct((rows, dw), hp.dtype),
        grid_spec=grid_spec,
        compiler_params=_params("arbitrary"),
        name="dispatch",
    )(code1, code2, offs, zero_block, zero_flag, hp)


def _expert_kernel(se_ref, sb_ref, sn_ref, ns_ref, xs_ref, wg_ref, wu_ref, wd_ref, o_ref,
                   hid_ref, *, n_up, sub):
    del se_ref, sb_ref
    s, p = pl.program_id(0), pl.program_id(1)
    valid = s < ns_ref[0]
    hsub = sub // 2
    n_half = sn_ref[s]
    n_full = n_half // 2
    odd = n_half % 2 == 1
    fc = wg_ref.shape[1]

    def rows_at(q_half, size):
        return pl.ds(pl.multiple_of(q_half * hsub, hsub), size)

    @pl.when(jnp.logical_and(valid, p < n_up))
    def _():
        wg = wg_ref[...].astype(BF16)
        wu = wu_ref[...].astype(BF16)

        def up(rows):
            x = xs_ref[rows, :].astype(BF16)
            a = _dot(x, wg)
            b = _dot(x, wu)
            hid_ref[p, rows, :] = (a * _sigmoid(a) * b).astype(BF16)

        def body(q, carry):
            up(rows_at(2 * q, sub))
            return carry

        lax.fori_loop(0, n_full, body, 0)

        @pl.when(odd)
        def _():
            up(rows_at(2 * n_full, hsub))

    @pl.when(jnp.logical_and(valid, p >= n_up))
    def _():
        wd = wd_ref[...].astype(BF16)

        def down(rows):
            y = _dot(hid_ref[0, rows, :], wd[0:fc])
            for j in range(1, n_up):
                y += _dot(hid_ref[j, rows, :], wd[j * fc:(j + 1) * fc])
            o_ref[rows, :] = y

        def body(q, carry):
            down(rows_at(2 * q, sub))
            return carry

        def zero_body(q, carry):
            o_ref[rows_at(q, hsub), :] = jnp.zeros((hsub, o_ref.shape[1]), o_ref.dtype)
            return carry

        lax.fori_loop(0, n_full, body, 0)

        @pl.when(odd)
        def _():
            down(rows_at(2 * n_full, hsub))

        lax.fori_loop(n_half, xs_ref.shape[0] // hsub, zero_body, 0)


def _experts(xs, w_gate, w_up, w_down, sup_expert, sup_block, sup_nsub, n_super, *, sub, sup):
    rows = xs.shape[0]
    _, d, f = w_gate.shape
    fc = _tile(f, 256)
    n_up = f // fc
    dc = _tile(d, min(1024, d // 2))
    n_down = d // dc
    n_phase = n_up + n_down

    def phase(s, p, ns):
        return jnp.where(s < ns[0], p, n_phase - 1)

    def x_map(s, p, se, sb, sn, ns):
        return (sb[s + (phase(s, p, ns) >= n_up).astype(jnp.int32)], 0)

    def up_map(first_down_phase):
        def index_map(s, p, se, sb, sn, ns):
            ph = phase(s, p, ns)
            ahead = ph >= n_up + first_down_phase
            return (se[s + ahead.astype(jnp.int32)], 0, jnp.where(ahead, 0, jnp.minimum(ph, n_up - 1)))
        return index_map

    def down_c(s, p, ns):
        return jnp.maximum(phase(s, p, ns) - n_up, 0)

    def down_map(s, p, se, sb, sn, ns):
        early = jnp.logical_and(phase(s, p, ns) < n_up - 1, s > 0)
        expert = se[jnp.where(early, s - 1, s)]
        return (expert, 0, jnp.where(early, n_down - 1, down_c(s, p, ns)))

    grid_spec = pltpu.PrefetchScalarGridSpec(
        num_scalar_prefetch=4,
        grid=(rows // sup, n_phase),
        in_specs=[pl.BlockSpec((sup, d), x_map),
                  pl.BlockSpec((None, d, fc), up_map(0)),
                  pl.BlockSpec((None, d, fc), up_map(1)),
                  pl.BlockSpec((None, f, dc), down_map)],
        out_specs=pl.BlockSpec((sup, dc), lambda s, p, se, sb, sn, ns: (sb[s], down_c(s, p, ns))),
        scratch_shapes=[pltpu.VMEM((n_up, sup, fc), BF16)],
    )
    return pl.pallas_call(
        functools.partial(_expert_kernel, n_up=n_up, sub=sub),
        out_shape=jax.ShapeDtypeStruct((rows, d), F32),
        grid_spec=grid_spec,
        compiler_params=_params("arbitrary", "arbitrary"),
        name="expert_mlp",
    )(sup_expert, sup_block, sup_nsub, n_super, xs, w_gate, w_up, w_down)


def _combine_kernel(c1_ref, c2_ref, offs_ref, x_ref, route_ref, g_ref, ys_ref, *rest, final):
    if final:
        y_ref, buf1, buf2, sem = rest
    else:
        x2_ref, hn_ref, buf1, buf2, sem = rest
    tr = x_ref.shape[0]
    step = pl.program_id(0)
    slot = step % 2
    out_ref = y_ref if final else hn_ref

    def copies_of(blk):
        s = blk % 2

        def copies(r):
            t = blk * tr + r
            return (_row_copy(ys_ref, _dest_row(c1_ref, offs_ref, t), buf1.at[s], r, sem.at[s, 0]),
                    _row_copy(ys_ref, _dest_row(c2_ref, offs_ref, t), buf2.at[s], r, sem.at[s, 1]))
        return copies

    @pl.when(step == 0)
    def _():
        _start_rows(copies_of(step), 0, tr)

    @pl.when(step + 1 < pl.num_programs(0))
    def _():
        _start_rows(copies_of(step + 1), 0, tr)

    _wait_rows(copies_of(step), 0, tr)

    route = route_ref[...]
    w1, w2 = route[:, 2:3], route[:, 3:4]
    x2 = x_ref[...] + w1 * buf1[slot] + w2 * buf2[slot]
    if not final:
        x2_ref[...] = x2
    out_ref[...] = _rms(x2, g_ref[...]).astype(out_ref.dtype)


def _combine(x, route, g, ys, code1, code2, offs, *, final):
    n, d = x.shape
    tr = _tile(n, 256)
    row = lambda i, *_: (i, 0)
    if final:
        out_shape = jax.ShapeDtypeStruct((n, d), F32)
        out_specs = pl.BlockSpec((tr, d), row)
    else:
        out_shape = (jax.ShapeDtypeStruct((n, d), F32), jax.ShapeDtypeStruct((n, d), BF16))
        out_specs = (pl.BlockSpec((tr, d), row), pl.BlockSpec((tr, d), row))
    grid_spec = pltpu.PrefetchScalarGridSpec(
        num_scalar_prefetch=3,
        grid=(n // tr,),
        in_specs=[pl.BlockSpec((tr, d), row),
                  pl.BlockSpec((tr, ROUTE_COLS), row),
                  pl.BlockSpec((1, d), lambda i, *_: (0, 0)),
                  pl.BlockSpec(memory_space=pl.ANY)],
        out_specs=out_specs,
        scratch_shapes=[pltpu.VMEM((2, tr, d), F32), pltpu.VMEM((2, tr, d), F32),
                        pltpu.SemaphoreType.DMA((2, 2))],
    )
    return pl.pallas_call(
        functools.partial(_combine_kernel, final=final),
        out_shape=out_shape,
        grid_spec=grid_spec,
        compiler_params=_params("arbitrary"),
        name="combine_final" if final else "combine",
    )(code1, code2, offs, x, route, g.reshape(1, d), ys)


def _plan(route, counts, *, ge, sub, sup, max_super):
    i32 = jnp.int32
    e1, e2 = route[:, 0].astype(i32), route[:, 1].astype(i32)
    rank1, rank2 = route[:, 4].astype(i32), route[:, 5].astype(i32)
    cnt = counts[0, :ge].astype(i32)
    n_sub = (cnt + sub - 1) // sub
    n_sup = (cnt + sup - 1) // sup
    ends = jnp.cumsum(n_sup * sup)
    offs = ends - n_sup * sup
    code1 = e1 * (1 << ROUTE_CODE_SHIFT) + rank1
    code2 = e2 * (1 << ROUTE_CODE_SHIFT) + rank2
    sup_ends = jnp.cumsum(n_sup)
    n_super = sup_ends[-1]
    s = jnp.arange(max_super + 1, dtype=i32)
    se = jnp.minimum(jnp.sum((sup_ends[None, :] <= s[:, None]).astype(i32), axis=1), ge - 1)
    k = s - (sup_ends[se] - n_sup[se])
    sb = offs[se] // sup + k
    sn = jnp.clip(n_sub[se] - k * (sup // sub), 0, sup // sub)
    last = jnp.maximum(n_super - 1, 0)
    valid = s < n_super
    sup_expert = jnp.where(valid, se, se[last]).astype(i32)
    sup_block = jnp.where(valid, sb, sb[last]).astype(i32)
    sup_nsub = jnp.where(valid, sn, 0).astype(i32)
    zero_block = (offs // sub + cnt // sub).astype(i32)
    zero_flag = (cnt % sub != 0).astype(i32)
    return (code1, code2, offs.astype(i32), sup_expert, sup_block, sup_nsub,
            n_super.reshape(1).astype(i32), zero_block, zero_flag)


def kernel(x, norm1_g, w_in, lb_logits, hgrn_norm_g, w_pool, pool_scale, w_branch_a, w_branch_b,
           w_out, norm2_g, w_router_group, w_router_expert, w_gate, w_up, w_down, final_norm_g):
    batch, seq, d = x.shape
    n = batch * seq
    depth = w_in.shape[0]
    width = lb_logits.shape[1]
    pool_width = pool_scale.shape[1]
    n_groups = w_router_group.shape[-1]
    n_experts = w_gate.shape[2]
    ge = n_groups * n_experts
    d_expert = w_gate.shape[-1]
    assert n_groups + ge <= LANES and n < (1 << ROUTE_CODE_SHIFT)
    col_pool = 4 * width
    col_ga = col_pool + pool_width
    col_gb = col_ga + d

    sub = MOE_SUB_ROWS if n >= 4096 else 32
    sup = MOE_HALVES_PER_SUPER * (sub // 2)
    max_super = (TOP_K * n + ge * (sup - 1)) // sup
    rows = max_super * sup

    wg_all = w_gate.reshape(depth * ge, d, d_expert)
    wu_all = w_up.reshape(depth * ge, d, d_expert)
    wd_all = w_down.reshape(depth * ge, d_expert, d)

    xf = x.reshape(n, d)
    h = _rmsnorm(xf, norm1_g[0], BF16)
    for l in range(depth):
        proj = _inproj(h, w_in, l)
        o_a = _hgrn(proj, lb_logits, hgrn_norm_g[l], layer=l, batch=batch, seq=seq, width=width)
        o_b = _pool(proj, w_pool, pool_scale, layer=l, batch=batch, seq=seq, col0=col_pool)
        merged = _merge(o_a, o_b, w_branch_a, w_branch_b, proj, layer=l, col_ga=col_ga, col_gb=col_gb)
        x1 = _outproj(merged, w_out, xf, l)

        w_router = jnp.concatenate([w_router_group[l], w_router_expert[l]], axis=1)
        w_router = jnp.pad(w_router, ((0, 0), (0, LANES - w_router.shape[1])))
        hp, route, counts = _router(x1, norm2_g[l], w_router, n_groups=n_groups, n_experts=n_experts)
        (code1, code2, offs, sup_expert, sup_block, sup_nsub, n_super, zero_block, zero_flag) = _plan(
            route, counts, ge=ge, sub=sub // 2, sup=sup, max_super=max_super)
        xs = _dispatch(hp, code1, code2, offs, zero_block, zero_flag, rows=rows, sub=sub // 2)
        ys = _experts(xs, wg_all, wu_all, wd_all, sup_expert + l * ge, sup_block, sup_nsub, n_super,
                      sub=sub, sup=sup)
        if l + 1 < depth:
            xf, h = _combine(x1, route, norm1_g[l + 1], ys, code1, code2, offs, final=False)
        else:
            out = _combine(x1, route, final_norm_g, ys, code1, code2, offs, final=True)
    return out.reshape(batch, seq, d)
```
